```python
import jax, jax.numpy as jnp
from jax import lax
import numpy as np

D_MODEL = 1024
BATCH = 8
SEQ = 4096
DEPTH = 4

N_META = 16
EPS = 1e-6
N_BRANCH = 4
POOL_WINDOWS = (2, 4, 8, 16)
POOL_GROUP = 64
POOL_W = POOL_GROUP * 4
MLA_HEADS = 8
QK_NOPE = 64
QK_ROPE = 32
V_DIM = 64
Q_RANK = 256
KV_RANK = 128
ROPE_THETA = 10000.0
MLA_W = MLA_HEADS * V_DIM
Q_BLOCK = 128
CONF_W = 256
CONF_K = 31
SC_W = 256
SC_K = 3

IN_SPLITS = (POOL_W, POOL_W,
             Q_RANK, KV_RANK, QK_ROPE, MLA_W,
             2 * CONF_W, CONF_W,
             3 * SC_W, SC_W,
             N_BRANCH * D_MODEL)
IN_W = sum(IN_SPLITS)

kernel_name = "hybrid_parallel_gated_mixers"


def rms_norm(x, g):
    xf = x.astype(jnp.float32)
    y = xf * lax.rsqrt(jnp.mean(xf * xf, axis=-1, keepdims=True) + EPS)
    return (y * g.astype(jnp.float32)).astype(x.dtype)


def layer_norm(x, g, b):
    xf = x.astype(jnp.float32)
    mu = jnp.mean(xf, axis=-1, keepdims=True)
    var = jnp.mean(jnp.square(xf - mu), axis=-1, keepdims=True)
    y = (xf - mu) * lax.rsqrt(var + EPS)
    return (y * g.astype(jnp.float32) + b.astype(jnp.float32)).astype(x.dtype)


def split_cols(z):
    idx = [int(i) for i in np.cumsum(IN_SPLITS)[:-1]]
    return jnp.split(z, idx, axis=-1)


def causal_dwconv(u, w):
    width, c = w.shape
    up = jnp.pad(u, ((0, 0), (width - 1, 0), (0, 0)))
    return lax.conv_general_dilated(up, w[:, None, :].astype(u.dtype), window_strides=(1,),
                                    padding='VALID', dimension_numbers=('NWC', 'WIO', 'NWC'),
                                    feature_group_count=c)


def rope_tables(n_pos, dim, dtype):
    inv = 1.0 / (ROPE_THETA ** (jnp.arange(0, dim, 2, dtype=jnp.float32) / dim))
    ang = jnp.arange(n_pos, dtype=jnp.float32)[:, None] * inv[None, :]
    return jnp.cos(ang).astype(dtype), jnp.sin(ang).astype(dtype)


def apply_rope(t, cos, sin):
    t1, t2 = jnp.split(t, 2, axis=-1)
    return jnp.concatenate([t1 * cos - t2 * sin, t1 * sin + t2 * cos], axis=-1)


def pool_mixer(v, w_group, scale):
    b_, l_, _ = v.shape
    vf = v.astype(jnp.float32)
    groups = jnp.split(vf, len(POOL_WINDOWS), axis=-1)
    pos = jnp.arange(l_)
    outs = []
    for g, w in zip(groups, POOL_WINDOWS):
        cs = jnp.cumsum(g, axis=1)
        lag = jnp.pad(cs, ((0, 0), (w, 0), (0, 0)))[:, :l_]
        cnt = jnp.minimum(pos + 1, w).astype(jnp.float32)[None, :, None]
        outs.append((cs - lag) / cnt - g)
    p = jnp.stack(outs, axis=2).astype(v.dtype)
    y = jnp.einsum('blgc,gcd->blgd', p, w_group).reshape(b_, l_, POOL_W)
    return y * scale


def mla_attention(c_q, c_kv, k_rope, q_norm_g, w_uq, kv_norm_g, w_ukv, cos, sin):
    b_, l_, _ = c_q.shape
    q = (rms_norm(c_q, q_norm_g) @ w_uq).reshape(b_, l_, MLA_HEADS, QK_NOPE + QK_ROPE)
    q_nope, q_rope = jnp.split(q, [QK_NOPE], axis=-1)
    q_rope = apply_rope(q_rope, cos[:, None, :], sin[:, None, :])
    kv = (rms_norm(c_kv, kv_norm_g) @ w_ukv).reshape(b_, l_, MLA_HEADS, QK_NOPE + V_DIM)
    k_nope, v = jnp.split(kv, [QK_NOPE], axis=-1)
    k_rope = apply_rope(k_rope, cos, sin)
    k = jnp.concatenate([k_nope, jnp.broadcast_to(k_rope[:, :, None, :], (b_, l_, MLA_HEADS, QK_ROPE))], axis=-1)
    qf = jnp.concatenate([q_nope, q_rope], axis=-1) * ((QK_NOPE + QK_ROPE) ** -0.5)
    n_blk = -(-l_ // Q_BLOCK)
    lp = n_blk * Q_BLOCK
    pad = ((0, 0), (0, lp - l_), (0, 0), (0, 0))
    qf, k, v = jnp.pad(qf, pad), jnp.pad(k, pad), jnp.pad(v, pad)
    k_pos = jnp.arange(lp)
    q_blocks = qf.reshape(b_, n_blk, Q_BLOCK, MLA_HEADS, QK_NOPE + QK_ROPE).transpose(1, 0, 2, 3, 4)

    def attend(args):
        qb, i = args
        s = jnp.einsum('bqhd,bkhd->bhqk', qb, k).astype(jnp.float32)
        q_pos = i * Q_BLOCK + jnp.arange(Q_BLOCK)
        s = jnp.where(k_pos[None, :] <= q_pos[:, None], s, -jnp.inf)
        p = jax.nn.softmax(s, axis=-1).astype(v.dtype)
        return jnp.einsum('bhqk,bkhd->bqhd', p, v)

    o = lax.map(attend, (q_blocks, jnp.arange(n_blk)))
    return o.transpose(1, 0, 2, 3, 4).reshape(b_, lp, MLA_W)[:, :l_]


def conformer_conv(u, w_dw, b_dw, ln_g, ln_b):
    a, gate = jnp.split(u, 2, axis=-1)
    z = a * jax.nn.sigmoid(gate)
    z = causal_dwconv(z, w_dw) + b_dw
    z = layer_norm(z, ln_g, ln_b)
    return jax.nn.silu(z)


def short_conv(bcx, w_dw):
    bg, cg, xv = jnp.split(bcx, 3, axis=-1)
    return bg * causal_dwconv(cg * xv, w_dw)


def setup_inputs(seed: int = 0) -> dict:
    key = jax.random.key(seed)
    ks = jax.random.split(key, 24)
    f32 = jnp.float32

    def nrm(k, shape, fan_in):
        return jax.random.normal(k, shape, f32) * (fan_in ** -0.5)

    def gain(k, shape):
        return 1.0 + 0.05 * jax.random.normal(k, shape, f32)

    def bias(k, shape):
        return 0.02 * jax.random.normal(k, shape, f32)

    return {
        "x": jax.random.normal(ks[0], (BATCH, SEQ, D_MODEL), f32),
        "meta_tokens": jax.random.normal(ks[1], (N_META, D_MODEL), f32),
        "pre_norm_g": gain(ks[2], (DEPTH, D_MODEL)),
        "w_in": nrm(ks[3], (DEPTH, D_MODEL, IN_W), D_MODEL),
        "gate_bias": bias(ks[4], (DEPTH, N_BRANCH * D_MODEL)),
        "pool_w": nrm(ks[5], (DEPTH, 4, POOL_GROUP, POOL_GROUP), POOL_GROUP),
        "pool_scale": gain(ks[6], (DEPTH, POOL_W)),
        "w_out_pool": nrm(ks[7], (DEPTH, POOL_W, D_MODEL), POOL_W),
        "q_norm_g": gain(ks[8], (DEPTH, Q_RANK)),
        "w_uq": nrm(ks[9], (DEPTH, Q_RANK, MLA_HEADS * (QK_NOPE + QK_ROPE)), Q_RANK),
        "kv_norm_g": gain(ks[10], (DEPTH, KV_RANK)),
        "w_ukv": nrm(ks[11], (DEPTH, KV_RANK, MLA_HEADS * (QK_NOPE + V_DIM)), KV_RANK),
        "w_out_mla": nrm(ks[12], (DEPTH, MLA_W, D_MODEL), MLA_W),
        "conf_dw_w": nrm(ks[13], (DEPTH, CONF_K, CONF_W), CONF_K),
        "conf_dw_b": bias(ks[14], (DEPTH, CONF_W)),
        "conf_ln_g": gain(ks[15], (DEPTH, CONF_W)),
        "conf_ln_b": bias(ks[16], (DEPTH, CONF_W)),
        "w_out_conf": nrm(ks[17], (DEPTH, CONF_W, D_MODEL), CONF_W),
        "sc_dw_w": nrm(ks[18], (DEPTH, SC_K, SC_W), SC_K),
        "w_out_sc": nrm(ks[19], (DEPTH, SC_W, D_MODEL), SC_W),
        "w_o": nrm(ks[20], (DEPTH, D_MODEL, D_MODEL), D_MODEL),
        "post_norm_g": gain(ks[21], (DEPTH, D_MODEL)),
    }


def reference(x, meta_tokens, pre_norm_g, w_in, gate_bias, pool_w, pool_scale, w_out_pool,
              q_norm_g, w_uq, kv_norm_g, w_ukv, w_out_mla, conf_dw_w, conf_dw_b, conf_ln_g,
              conf_ln_b, w_out_conf, sc_dw_w, w_out_sc, w_o, post_norm_g):
    b_ = x.shape[0]
    meta = jnp.broadcast_to(meta_tokens[None].astype(x.dtype), (b_, N_META, D_MODEL))
    h_res = jnp.concatenate([meta, x], axis=1)
    l_ = h_res.shape[1]
    cos, sin = rope_tables(l_, QK_ROPE, x.dtype)

    for i in range(DEPTH):
        h = rms_norm(h_res, pre_norm_g[i])
        z = h @ w_in[i]
        (pv, pg, cq, ckv, kr, mg, cu, cg, sbcx, sg, gl) = split_cols(z)

        y_a = (pool_mixer(pv, pool_w[i], pool_scale[i]) * jax.nn.silu(pg)) @ w_out_pool[i]
        y_b = (mla_attention(cq, ckv, kr, q_norm_g[i], w_uq[i], kv_norm_g[i], w_ukv[i], cos, sin)
               * jax.nn.silu(mg)) @ w_out_mla[i]
        y_c = (conformer_conv(cu, conf_dw_w[i], conf_dw_b[i], conf_ln_g[i], conf_ln_b[i])
               * jax.nn.silu(cg)) @ w_out_conf[i]
        y_d = (short_conv(sbcx, sc_dw_w[i]) * jax.nn.silu(sg)) @ w_out_sc[i]

        gts = jax.nn.sigmoid(gl + gate_bias[i]).reshape(b_, l_, N_BRANCH, D_MODEL)
        m = (gts[:, :, 0] * y_a + gts[:, :, 1] * y_b + gts[:, :, 2] * y_c + gts[:, :, 3] * y_d)
        h_res = h_res + rms_norm(m @ w_o[i], post_norm_g[i])

    return h_res[:, N_META:]
```

```python
import functools

import jax
import jax.numpy as jnp
import numpy as np
from jax import lax
from jax.experimental import pallas as pl
from jax.experimental.pallas import tpu as pltpu

F32 = jnp.float32
BF16 = jnp.bfloat16

D_MODEL = 1024
DEPTH = 4
N_META = 16
EPS = 1e-6
N_BRANCH = 4
POOL_WINDOWS = (2, 4, 8, 16)
POOL_GROUP = 64
POOL_W = 256
MLA_HEADS = 8
QK_NOPE = 64
QK_ROPE = 32
V_DIM = 64
Q_RANK = 256
KV_RANK = 128
ROPE_THETA = 10000.0
MLA_W = MLA_HEADS * V_DIM
CONF_W = 256
CONF_K = 31
SC_W = 256
SC_K = 3
IN_SPLITS = (POOL_W, POOL_W, Q_RANK, KV_RANK, QK_ROPE, MLA_W, 2 * CONF_W, CONF_W, 3 * SC_W, SC_W,
             N_BRANCH * D_MODEL)

LANES = 128
HEAD_PAD = LANES
ROW_TILE = 256
POOL_HALO = 16
CONF_HALO = 32
SC_HALO = 8
VMEM_LIMIT = 56 * 1024 * 1024

C_PV, C_PG, C_CQ, C_CKV, C_KR, C_KRS, C_MG, C_CU, C_CG, C_SB, C_SG, C_END = (
    0, 256, 512, 768, 896, 1024, 1152, 1664, 2176, 2432, 3200, 3456)


def _rms(x, g):
    return x * lax.rsqrt(jnp.mean(x * x, axis=-1, keepdims=True) + EPS) * g


def _dot(a, b):
    return jnp.dot(a, b, preferred_element_type=F32)


def _pre_kernel(x_ref, png_ref, w1_ref, pbd_ref, psc_ref, qng_ref, wq_ref, wqs_ref, kvg_ref, wk_ref,
                wv_ref, cq_ref, sq_ref, ck_ref, sk_ref, cw_ref, cb_ref, lg_ref, lb_ref, sw_ref,
                q_ref, k_ref, v_ref, smg_ref, u_ref, pbuf, cbuf, sbuf):
    tm = ROW_TILE
    t = pl.program_id(1)

    @pl.when(t == 0)
    def _():
        pbuf[0:POOL_HALO, :] = jnp.zeros((POOL_HALO, POOL_W), F32)
        cbuf[0:CONF_HALO, :] = jnp.zeros((CONF_HALO, CONF_W), F32)
        sbuf[0:SC_HALO, :] = jnp.zeros((SC_HALO, SC_W), F32)

    h = _rms(x_ref[0], png_ref[...]).astype(BF16)

    def proj(lo, hi):
        return _dot(h, w1_ref[:, lo:hi])

    zp = proj(C_PV, C_CQ)
    pv = zp[:, :POOL_W]
    pbuf[POOL_HALO:POOL_HALO + tm, :] = pv
    pos = lax.broadcasted_iota(jnp.int32, (tm, LANES), 0) + t * tm
    low_half = lax.broadcasted_iota(jnp.int32, (tm, LANES), 1) < POOL_GROUP

    def window_sum(col, acc, j_lo, j_hi):
        for j in range(j_lo, j_hi):
            acc = acc + pbuf[POOL_HALO - j:POOL_HALO - j + tm, col:col + LANES]
        return acc

    pooled = []
    for tile, (w_lo, w_hi) in enumerate(((POOL_WINDOWS[0], POOL_WINDOWS[1]),
                                         (POOL_WINDOWS[2], POOL_WINDOWS[3]))):
        col = tile * LANES
        self_v = pv[:, col:col + LANES]
        s_lo = window_sum(col, self_v, 1, w_lo)
        s_hi = window_sum(col, s_lo, w_lo, w_hi)
        width = jnp.where(low_half, w_lo, w_hi)
        cnt = jnp.minimum(pos + 1, width).astype(F32)
        pooled.append(jnp.where(low_half, s_lo, s_hi) / cnt - self_v)
    p = jnp.concatenate(pooled, axis=1).astype(BF16)
    ya = _dot(p, pbd_ref[...]) * psc_ref[...] * jax.nn.silu(zp[:, POOL_W:])
    u_ref[0, :, 0:POOL_W] = ya.astype(BF16)
    pbuf[0:POOL_HALO, :] = pbuf[tm:tm + POOL_HALO, :]

    zc = proj(C_CQ, C_MG)
    cqn = _rms(zc[:, 0:Q_RANK], qng_ref[...]).astype(BF16)
    qa = _dot(cqn, wq_ref[...])
    qb = _dot(cqn, wqs_ref[...])
    cq_t = cq_ref[...]
    sq_t = sq_ref[...]
    for hd in range(MLA_HEADS):
        sl = slice(hd * HEAD_PAD, (hd + 1) * HEAD_PAD)
        q_ref[0, :, sl] = (qa[:, sl] * cq_t + qb[:, sl] * sq_t).astype(BF16)
    ckvn = _rms(zc[:, Q_RANK:Q_RANK + KV_RANK], kvg_ref[...]).astype(BF16)
    kn = _dot(ckvn, wk_ref[...])
    k_rope = (zc[:, C_KR - C_CQ:C_KRS - C_CQ] * ck_ref[...]
              + zc[:, C_KRS - C_CQ:C_MG - C_CQ] * sk_ref[...])
    for hd in range(MLA_HEADS):
        sl = slice(hd * HEAD_PAD, (hd + 1) * HEAD_PAD)
        k_ref[0, :, sl] = (kn[:, sl] + k_rope).astype(BF16)
    v_ref[0] = _dot(ckvn, wv_ref[...]).astype(BF16)
    smg_ref[0] = jax.nn.silu(proj(C_MG, C_CU)).astype(BF16)

    zu = proj(C_CU, C_SB)
    glu = zu[:, 0:CONF_W] * jax.nn.sigmoid(zu[:, CONF_W:2 * CONF_W])
    cbuf[CONF_HALO:CONF_HALO + tm, :] = glu
    acc = jnp.zeros((tm, CONF_W), F32) + cb_ref[...]
    base = CONF_HALO - (CONF_K - 1)
    for kk in range(CONF_K):
        acc = acc + cw_ref[kk:kk + 1, :] * cbuf[base + kk:base + kk + tm, :]
    mu = jnp.mean(acc, axis=-1, keepdims=True)
    cen = acc - mu
    var = jnp.mean(cen * cen, axis=-1, keepdims=True)
    yc = cen * lax.rsqrt(var + EPS) * lg_ref[...] + lb_ref[...]
    yc = jax.nn.silu(yc) * jax.nn.silu(zu[:, 2 * CONF_W:3 * CONF_W])
    u_ref[0, :, POOL_W:POOL_W + CONF_W] = yc.astype(BF16)
    cbuf[0:CONF_HALO, :] = cbuf[tm:tm + CONF_HALO, :]

    zs = proj(C_SB, C_END)
    sbuf[SC_HALO:SC_HALO + tm, :] = zs[:, SC_W:2 * SC_W] * zs[:, 2 * SC_W:3 * SC_W]
    conv = jnp.zeros((tm, SC_W), F32)
    base = SC_HALO - (SC_K - 1)
    for kk in range(SC_K):
        conv = conv + sw_ref[kk:kk + 1, :] * sbuf[base + kk:base + kk + tm, :]
    yd = zs[:, 0:SC_W] * conv * jax.nn.silu(zs[:, 3 * SC_W:4 * SC_W])
    u_ref[0, :, POOL_W + CONF_W:POOL_W + CONF_W + SC_W] = yd.astype(BF16)
    sbuf[0:SC_HALO, :] = sbuf[tm:tm + SC_HALO, :]


def _const_spec(shape):
    return pl.BlockSpec(shape, lambda b, t: (0,) * len(shape))


def _pre_call(x, lw, tabs, lp):
    bsz = x.shape[0]
    nt = lp // ROW_TILE
    row_spec = lambda w: pl.BlockSpec((1, ROW_TILE, w), lambda b, t: (b, t, 0))
    tab_spec = pl.BlockSpec((ROW_TILE, LANES), lambda b, t: (t, 0))
    consts = [lw["png"], lw["w1"], lw["pbd"], lw["psc"], lw["qng"], lw["wq"], lw["wqs"], lw["kvg"],
              lw["wk"], lw["wv"]]
    consts2 = [lw["cw"], lw["cb"], lw["lg"], lw["lb"], lw["sw"]]
    in_specs = ([row_spec(D_MODEL)] + [_const_spec(c.shape) for c in consts] + [tab_spec] * 4
                + [_const_spec(c.shape) for c in consts2])
    widths = (MLA_HEADS * HEAD_PAD, MLA_HEADS * HEAD_PAD, MLA_W, MLA_W, POOL_W + CONF_W + SC_W)
    return pl.pallas_call(
        _pre_kernel,
        grid=(bsz, nt),
        in_specs=in_specs,
        out_specs=[row_spec(w) for w in widths],
        out_shape=[jax.ShapeDtypeStruct((bsz, lp, w), BF16) for w in widths],
        scratch_shapes=[pltpu.VMEM((POOL_HALO + ROW_TILE, POOL_W), F32),
                        pltpu.VMEM((CONF_HALO + ROW_TILE, CONF_W), F32),
                        pltpu.VMEM((SC_HALO + ROW_TILE, SC_W), F32)],
        compiler_params=pltpu.CompilerParams(dimension_semantics=("arbitrary", "arbitrary"),
                                             vmem_limit_bytes=VMEM_LIMIT),
        name="mixer_pre",
    )(x, *consts, *tabs, *consts2)


def _attn_kernel(q_ref, k_ref, v_ref, o_ref):
    tq = ROW_TILE
    i = pl.program_id(1)
    row = lax.broadcasted_iota(jnp.int32, (tq, tq), 0)
    col = lax.broadcasted_iota(jnp.int32, (tq, tq), 1)
    causal = col <= row
    low_half = lax.broadcasted_iota(jnp.int32, (tq, LANES), 1) < V_DIM

    for pair in range(MLA_HEADS // 2):
        outs = []
        for hd in (2 * pair, 2 * pair + 1):
            hsl = slice(hd * HEAD_PAD, (hd + 1) * HEAD_PAD)
            vsl = slice(pair * LANES, (pair + 1) * LANES)
            q = q_ref[0, :, hsl]

            def block(j, carry, masked, hsl=hsl, vsl=vsl, q=q):
                m, l, acc = carry
                start = pl.multiple_of(j * tq, tq)
                kj = k_ref[0, pl.ds(start, tq), hsl]
                vj = v_ref[0, pl.ds(start, tq), vsl]
                s = lax.dot_general(q, kj, (((1,), (1,)), ((), ())), preferred_element_type=F32)
                if masked:
                    s = jnp.where(causal, s, -jnp.inf)
                m_new = jnp.maximum(m, jnp.max(s, axis=-1, keepdims=True))
                alpha = jnp.exp(m - m_new)
                p = jnp.exp(s - m_new)
                l = alpha * l + jnp.sum(p, axis=-1, keepdims=True)
                acc = alpha * acc + _dot(p.astype(BF16), vj)
                return m_new, l, acc

            init = (jnp.full((tq, 1), -jnp.inf, F32), jnp.zeros((tq, 1), F32),
                    jnp.zeros((tq, LANES), F32))
            carry = lax.fori_loop(0, i, functools.partial(block, masked=False), init)
            _, l, acc = block(i, carry, True)
            outs.append(acc / l)
        o_ref[0, :, pair * LANES:(pair + 1) * LANES] = jnp.where(low_half, outs[0], outs[1]).astype(BF16)


def _attn_call(q, k, v, lp):
    bsz = q.shape[0]
    nq = lp // ROW_TILE
    return pl.pallas_call(
        _attn_kernel,
        grid=(bsz, nq),
        in_specs=[pl.BlockSpec((1, ROW_TILE, MLA_HEADS * HEAD_PAD), lambda b, i: (b, i, 0)),
                  pl.BlockSpec((1, lp, MLA_HEADS * HEAD_PAD), lambda b, i: (b, 0, 0)),
                  pl.BlockSpec((1, lp, MLA_W), lambda b, i: (b, 0, 0))],
        out_specs=pl.BlockSpec((1, ROW_TILE, MLA_W), lambda b, i: (b, i, 0)),
        out_shape=jax.ShapeDtypeStruct((bsz, lp, MLA_W), BF16),
        compiler_params=pltpu.CompilerParams(dimension_semantics=("arbitrary", "arbitrary"),
                                             vmem_limit_bytes=VMEM_LIMIT),
        name="mla_attention",
    )(q, k, v)


def _post_kernel(x_ref, u_ref, o_ref, smg_ref, png_ref, wg_ref, gb_ref, wpool_ref, wmla_ref, wconf_ref,
                 wsc_ref, wo_ref, pog_ref, out_ref):
    x = x_ref[...]
    h = _rms(x, png_ref[...]).astype(BF16)
    u = u_ref[...]
    ub = (o_ref[...].astype(F32) * smg_ref[...].astype(F32)).astype(BF16)
    branch_in = (u[:, 0:POOL_W], ub, u[:, POOL_W:POOL_W + CONF_W], u[:, POOL_W + CONF_W:])
    branch_w = (wpool_ref, wmla_ref, wconf_ref, wsc_ref)
    m = None
    for br in range(N_BRANCH):
        csl = slice(br * D_MODEL, (br + 1) * D_MODEL)
        gate = jax.nn.sigmoid(_dot(h, wg_ref[:, csl]) + gb_ref[:, csl])
        y = gate * _dot(branch_in[br], branch_w[br][...])
        m = y if m is None else m + y
    mo = _dot(m.astype(BF16), wo_ref[...])
    out_ref[...] = x + _rms(mo, pog_ref[...])


def _post_call(x2, u2, o2, smg2, lw):
    rows = x2.shape[0]
    tm = 2 * ROW_TILE
    row_spec = lambda w: pl.BlockSpec((tm, w), lambda r: (r, 0))
    consts = [lw["png"], lw["wg"], lw["gb"], lw["wpool"], lw["wmla"], lw["wconf"], lw["wsc"], lw["wo"],
              lw["pog"]]
    const_specs = [pl.BlockSpec(c.shape, lambda r, n=len(c.shape): (0,) * n) for c in consts]
    return pl.pallas_call(
        _post_kernel,
        grid=(rows // tm,),
        in_specs=[row_spec(D_MODEL), row_spec(POOL_W + CONF_W + SC_W), row_spec(MLA_W), row_spec(MLA_W)]
        + const_specs,
        out_specs=row_spec(D_MODEL),
        out_shape=jax.ShapeDtypeStruct((rows, D_MODEL), F32),
        compiler_params=pltpu.CompilerParams(dimension_semantics=("arbitrary",),
                                             vmem_limit_bytes=VMEM_LIMIT),
        name="mixer_post",
    )(x2, u2, o2, smg2, *consts)


def _rope_tables(lp):
    inv = 1.0 / (ROPE_THETA ** (jnp.arange(0, QK_ROPE, 2, dtype=F32) / QK_ROPE))
    ang = jnp.arange(lp, dtype=F32)[:, None] * inv[None, :]
    cos, sin = jnp.cos(ang), jnp.sin(ang)
    zeros_tail = jnp.zeros((lp, HEAD_PAD - QK_NOPE - QK_ROPE), F32)
    c_rope = jnp.concatenate([cos, cos], axis=1)
    s_rope = jnp.concatenate([-sin, sin], axis=1)
    scale = (QK_NOPE + QK_ROPE) ** -0.5
    cq = jnp.concatenate([jnp.ones((lp, QK_NOPE), F32), c_rope, zeros_tail], axis=1) * scale
    sq = jnp.concatenate([jnp.zeros((lp, QK_NOPE), F32), s_rope, zeros_tail], axis=1) * scale
    ck = jnp.concatenate([jnp.zeros((lp, QK_NOPE), F32), c_rope, zeros_tail], axis=1)
    sk = jnp.concatenate([jnp.zeros((lp, QK_NOPE), F32), s_rope, zeros_tail], axis=1)
    return cq, sq, ck, sk


def _swap_halves(w):
    half = QK_ROPE // 2
    return jnp.concatenate([w[..., half:], w[..., :half]], axis=-1)


def _layer_weights(i, w_in, p):
    idx = [int(s) for s in np.cumsum(IN_SPLITS)[:-1]]
    (w_pv, w_pg, w_cq, w_ckv, w_kr, w_mg, w_cu, w_cg, w_sb, w_sg, w_gl) = jnp.split(w_in[i], idx, axis=-1)
    zpad = lambda n: jnp.zeros((D_MODEL, n), F32)
    tail = HEAD_PAD - QK_NOPE - QK_ROPE
    kr_tile = jnp.concatenate([zpad(QK_NOPE), w_kr, zpad(tail)], axis=1)
    krs_tile = jnp.concatenate([zpad(QK_NOPE), _swap_halves(w_kr), zpad(tail)], axis=1)
    w1 = jnp.concatenate([w_pv, w_pg, w_cq, w_ckv, kr_tile, krs_tile, w_mg, w_cu, w_cg, w_sb, w_sg],
                         axis=1).astype(BF16)

    pbd = jnp.zeros((POOL_W, POOL_W), F32)
    for g in range(len(POOL_WINDOWS)):
        sl = slice(g * POOL_GROUP, (g + 1) * POOL_GROUP)
        pbd = pbd.at[sl, sl].set(p["pool_w"][i, g])

    wuq = p["w_uq"][i].reshape(Q_RANK, MLA_HEADS, QK_NOPE + QK_ROPE)
    zq = lambda n: jnp.zeros((Q_RANK, MLA_HEADS, n), F32)
    wq = jnp.concatenate([wuq, zq(tail)], axis=-1).reshape(Q_RANK, MLA_HEADS * HEAD_PAD)
    wqs = jnp.concatenate([zq(QK_NOPE), _swap_halves(wuq[..., QK_NOPE:]), zq(tail)],
                          axis=-1).reshape(Q_RANK, MLA_HEADS * HEAD_PAD)
    wukv = p["w_ukv"][i].reshape(KV_RANK, MLA_HEADS, QK_NOPE + V_DIM)
    wk = jnp.concatenate([wukv[..., :QK_NOPE], jnp.zeros((KV_RANK, MLA_HEADS, HEAD_PAD - QK_NOPE), F32)],
                         axis=-1).reshape(KV_RANK, MLA_HEADS * HEAD_PAD)
    wv = wukv[..., QK_NOPE:].reshape(KV_RANK, MLA_W)
    row = lambda a: a[i][None, :]
    return dict(
        png=row(p["pre_norm_g"]), w1=w1, pbd=pbd.astype(BF16), psc=row(p["pool_scale"]),
        qng=row(p["q_norm_g"]), wq=wq.astype(BF16), wqs=wqs.astype(BF16), kvg=row(p["kv_norm_g"]),
        wk=wk.astype(BF16), wv=wv.astype(BF16), cw=p["conf_dw_w"][i], cb=row(p["conf_dw_b"]),
        lg=row(p["conf_ln_g"]), lb=row(p["conf_ln_b"]), sw=p["sc_dw_w"][i],
        wg=w_gl.astype(BF16), gb=row(p["gate_bias"]), wpool=p["w_out_pool"][i].astype(BF16),
        wmla=p["w_out_mla"][i].astype(BF16), wconf=p["w_out_conf"][i].astype(BF16),
        wsc=p["w_out_sc"][i].astype(BF16), wo=p["w_o"][i].astype(BF16), pog=row(p["post_norm_g"]))


def kernel(x, meta_tokens, pre_norm_g, w_in, gate_bias, pool_w, pool_scale, w_out_pool, q_norm_g, w_uq,
           kv_norm_g, w_ukv, w_out_mla, conf_dw_w, conf_dw_b, conf_ln_g, conf_ln_b, w_out_conf, sc_dw_w,
           w_out_sc, w_o, post_norm_g):
    bsz, seq, _ = x.shape
    l_real = N_META + seq
    lp = pl.cdiv(l_real, ROW_TILE) * ROW_TILE
    assert (bsz * lp) % (2 * ROW_TILE) == 0
    params = dict(pre_norm_g=pre_norm_g, gate_bias=gate_bias, pool_w=pool_w, pool_scale=pool_scale,
                  w_out_pool=w_out_pool, q_norm_g=q_norm_g, w_uq=w_uq, kv_norm_g=kv_norm_g, w_ukv=w_ukv,
                  w_out_mla=w_out_mla, conf_dw_w=conf_dw_w, conf_dw_b=conf_dw_b, conf_ln_g=conf_ln_g,
                  conf_ln_b=conf_ln_b, w_out_conf=w_out_conf, sc_dw_w=sc_dw_w, w_out_sc=w_out_sc, w_o=w_o,
                  post_norm_g=post_norm_g)
    meta = jnp.broadcast_to(meta_tokens[None].astype(x.dtype), (bsz, N_META, D_MODEL))
    h_res = jnp.concatenate([meta, x, jnp.zeros((bsz, lp - l_real, D_MODEL), x.dtype)], axis=1)
    tabs = _rope_tables(lp)

    for i in range(DEPTH):
        lw = _layer_weights(i, w_in, params)
        q, k, v, smg, u = _pre_call(h_res, lw, tabs, lp)
        o = _attn_call(q, k, v, lp)
        flat = lambda a: a.reshape(bsz * lp, a.shape[-1])
        h_res = _post_call(flat(h_res), flat(u), flat(o), flat(smg), lw).reshape(bsz, lp, D_MODEL)

    return h_res[:, N_META:l_real]
```

```python
import functools

import jax
import jax.numpy as jnp
import numpy as np
from jax import lax
from jax.experimental import pallas as pl
from jax.experimental.pallas import tpu as pltpu

F32 = jnp.float32
BF16 = jnp.bfloat16

D_MODEL = 1024
DEPTH = 4
N_META = 16
EPS = 1e-6
N_BRANCH = 4
POOL_WINDOWS = (2, 4, 8, 16)
POOL_GROUP = 64
POOL_W = 256
MLA_HEADS = 8
QK_NOPE = 64
QK_ROPE = 32
V_DIM = 64
Q_RANK = 256
KV_RANK = 128
ROPE_THETA = 10000.0
MLA_W = MLA_HEADS * V_DIM
CONF_W = 256
CONF_K = 31
SC_W = 256
SC_K = 3
IN_SPLITS = (POOL_W, POOL_W, Q_RANK, KV_RANK, QK_ROPE, MLA_W, 2 * CONF_W, CONF_W, 3 * SC_W, SC_W,
             N_BRANCH * D_MODEL)

LANES = 128
HEAD_PAD = LANES
ROW_TILE = 256
POOL_HALO = 16
CONF_HALO = 32
SC_HALO = 8
VMEM_LIMIT = 56 * 1024 * 1024

C_PV, C_PG, C_CQ, C_CKV, C_KR, C_KRS, C_MG, C_CU, C_CG, C_SB, C_SG, C_END = (
    0, 256, 512, 768, 896, 1024, 1152, 1664, 2176, 2432, 3200, 3456)


def _rms(x, g):
    return x * lax.rsqrt(jnp.mean(x * x, axis=-1, keepdims=True) + EPS) * g


def _dot(a, b):
    return jnp.dot(a, b, preferred_element_type=F32)


def _dot_nt(a, b):
    return lax.dot_general(a, b, (((1,), (1,)), ((), ())), preferred_element_type=F32)


def _pre_kernel(x_ref, png_ref, w1_ref, pbd_ref, psc_ref, qng_ref, wq_ref, wqs_ref, kvg_ref, wk_ref,
                wv_ref, cq_ref, sq_ref, ck_ref, sk_ref, cw_ref, cb_ref, lg_ref, lb_ref, sw_ref,
                q_ref, k_ref, v_ref, smg_ref, u_ref, pbuf, cbuf, sbuf):
    tm = ROW_TILE
    t = pl.program_id(1)

    @pl.when(t == 0)
    def _():
        pbuf[0:POOL_HALO, :] = jnp.zeros((POOL_HALO, POOL_W), F32)
        cbuf[0:CONF_HALO, :] = jnp.zeros((CONF_HALO, CONF_W), F32)
        sbuf[0:SC_HALO, :] = jnp.zeros((SC_HALO, SC_W), F32)

    h = _rms(x_ref[0], png_ref[...]).astype(BF16)

    def proj(lo, hi):
        return _dot(h, w1_ref[:, lo:hi])

    zp = proj(C_PV, C_CQ)
    pv = zp[:, :POOL_W]
    pbuf[POOL_HALO:POOL_HALO + tm, :] = pv
    pos = lax.broadcasted_iota(jnp.int32, (tm, LANES), 0) + t * tm
    low_half = lax.broadcasted_iota(jnp.int32, (tm, LANES), 1) < POOL_GROUP

    def window_sum(col, acc, j_lo, j_hi):
        for j in range(j_lo, j_hi):
            acc = acc + pbuf[POOL_HALO - j:POOL_HALO - j + tm, col:col + LANES]
        return acc

    pooled = []
    for tile, (w_lo, w_hi) in enumerate(((POOL_WINDOWS[0], POOL_WINDOWS[1]),
                                         (POOL_WINDOWS[2], POOL_WINDOWS[3]))):
        col = tile * LANES
        self_v = pv[:, col:col + LANES]
        s_lo = window_sum(col, self_v, 1, w_lo)
        s_hi = window_sum(col, s_lo, w_lo, w_hi)
        width = jnp.where(low_half, w_lo, w_hi)
        cnt = jnp.minimum(pos + 1, width).astype(F32)
        pooled.append(jnp.where(low_half, s_lo, s_hi) / cnt - self_v)
    p = jnp.concatenate(pooled, axis=1).astype(BF16)
    ya = _dot(p, pbd_ref[...]) * psc_ref[...] * jax.nn.silu(zp[:, POOL_W:])
    u_ref[0, :, 0:POOL_W] = ya.astype(BF16)
    pbuf[0:POOL_HALO, :] = pbuf[tm:tm + POOL_HALO, :]

    zc = proj(C_CQ, C_MG)
    cqn = _rms(zc[:, 0:Q_RANK], qng_ref[...]).astype(BF16)
    qa = _dot_nt(wq_ref[...], cqn)
    qb = _dot_nt(wqs_ref[...], cqn)
    cq_t = cq_ref[...]
    sq_t = sq_ref[...]
    for hd in range(MLA_HEADS):
        sl = slice(hd * HEAD_PAD, (hd + 1) * HEAD_PAD)
        q_ref[0, sl, :] = (qa[sl, :] * cq_t + qb[sl, :] * sq_t).astype(BF16)
    ckvn = _rms(zc[:, Q_RANK:Q_RANK + KV_RANK], kvg_ref[...]).astype(BF16)
    kn = _dot(ckvn, wk_ref[...])
    k_rope = (zc[:, C_KR - C_CQ:C_KRS - C_CQ] * ck_ref[...]
              + zc[:, C_KRS - C_CQ:C_MG - C_CQ] * sk_ref[...])
    for hd in range(MLA_HEADS):
        sl = slice(hd * HEAD_PAD, (hd + 1) * HEAD_PAD)
        k_ref[0, 0, :, sl] = (kn[:, sl] + k_rope).astype(BF16)
    v_ref[0, 0] = _dot_nt(wv_ref[...], ckvn).astype(BF16)
    smg_ref[0] = jax.nn.silu(proj(C_MG, C_CU)).astype(BF16)

    zu = proj(C_CU, C_SB)
    glu = zu[:, 0:CONF_W] * jax.nn.sigmoid(zu[:, CONF_W:2 * CONF_W])
    cbuf[CONF_HALO:CONF_HALO + tm, :] = glu
    acc = jnp.zeros((tm, CONF_W), F32) + cb_ref[...]
    base = CONF_HALO - (CONF_K - 1)
    for kk in range(CONF_K):
        acc = acc + cw_ref[kk:kk + 1, :] * cbuf[base + kk:base + kk + tm, :]
    mu = jnp.mean(acc, axis=-1, keepdims=True)
    cen = acc - mu
    var = jnp.mean(cen * cen, axis=-1, keepdims=True)
    yc = cen * lax.rsqrt(var + EPS) * lg_ref[...] + lb_ref[...]
    yc = jax.nn.silu(yc) * jax.nn.silu(zu[:, 2 * CONF_W:3 * CONF_W])
    u_ref[0, :, POOL_W:POOL_W + CONF_W] = yc.astype(BF16)
    cbuf[0:CONF_HALO, :] = cbuf[tm:tm + CONF_HALO, :]

    zs = proj(C_SB, C_END)
    sbuf[SC_HALO:SC_HALO + tm, :] = zs[:, SC_W:2 * SC_W] * zs[:, 2 * SC_W:3 * SC_W]
    conv = jnp.zeros((tm, SC_W), F32)
    base = SC_HALO - (SC_K - 1)
    for kk in range(SC_K):
        conv = conv + sw_ref[kk:kk + 1, :] * sbuf[base + kk:base + kk + tm, :]
    yd = zs[:, 0:SC_W] * conv * jax.nn.silu(zs[:, 3 * SC_W:4 * SC_W])
    u_ref[0, :, POOL_W + CONF_W:POOL_W + CONF_W + SC_W] = yd.astype(BF16)
    sbuf[0:SC_HALO, :] = sbuf[tm:tm + SC_HALO, :]


def _const_spec(shape):
    return pl.BlockSpec(shape, lambda b, t: (0,) * len(shape))


def _pre_call(x, lw, tabs, lp):
    bsz = x.shape[0]
    nt = lp // ROW_TILE
    row_spec = lambda w: pl.BlockSpec((1, ROW_TILE, w), lambda b, t: (b, t, 0))
    tab_spec = pl.BlockSpec((ROW_TILE, LANES), lambda b, t: (t, 0))
    tab_t_spec = pl.BlockSpec((LANES, ROW_TILE), lambda b, t: (0, t))
    consts = [lw["png"], lw["w1"], lw["pbd"], lw["psc"], lw["qng"], lw["wq"], lw["wqs"], lw["kvg"],
              lw["wk"], lw["wv"]]
    consts2 = [lw["cw"], lw["cb"], lw["lg"], lw["lb"], lw["sw"]]
    in_specs = ([row_spec(D_MODEL)] + [_const_spec(c.shape) for c in consts]
                + [tab_t_spec, tab_t_spec, tab_spec, tab_spec] + [_const_spec(c.shape) for c in consts2])
    qk_w = MLA_HEADS * HEAD_PAD
    u_w = POOL_W + CONF_W + SC_W
    out_specs = [pl.BlockSpec((1, qk_w, ROW_TILE), lambda b, t: (b, 0, t)),
                 pl.BlockSpec((1, 1, ROW_TILE, qk_w), lambda b, t: (b, t, 0, 0)),
                 pl.BlockSpec((1, 1, MLA_W, ROW_TILE), lambda b, t: (b, t, 0, 0)),
                 row_spec(MLA_W), row_spec(u_w)]
    out_shape = [jax.ShapeDtypeStruct((bsz, qk_w, lp), BF16),
                 jax.ShapeDtypeStruct((bsz, nt, ROW_TILE, qk_w), BF16),
                 jax.ShapeDtypeStruct((bsz, nt, MLA_W, ROW_TILE), BF16),
                 jax.ShapeDtypeStruct((bsz, lp, MLA_W), BF16),
                 jax.ShapeDtypeStruct((bsz, lp, u_w), BF16)]
    return pl.pallas_call(
        _pre_kernel,
        grid=(bsz, nt),
        in_specs=in_specs,
        out_specs=out_specs,
        out_shape=out_shape,
        scratch_shapes=[pltpu.VMEM((POOL_HALO + ROW_TILE, POOL_W), F32),
                        pltpu.VMEM((CONF_HALO + ROW_TILE, CONF_W), F32),
                        pltpu.VMEM((SC_HALO + ROW_TILE, SC_W), F32)],
        compiler_params=pltpu.CompilerParams(dimension_semantics=("arbitrary", "arbitrary"),
                                             vmem_limit_bytes=VMEM_LIMIT),
        name="mixer_pre",
    )(x, *consts, *tabs, *consts2)


def _attn_kernel(qt_ref, k_ref, vt_ref, o_ref, acc_sc):
    tq = ROW_TILE
    i = pl.program_id(1)
    key_idx = lax.broadcasted_iota(jnp.int32, (tq, tq), 0)
    qry_idx = lax.broadcasted_iota(jnp.int32, (tq, tq), 1)
    causal = key_idx <= qry_idx

    acc_sc[...] = jnp.zeros(acc_sc.shape, F32)

    def block(j, stats, masked):
        new_stats = []
        scores = []
        for hd in range(MLA_HEADS):
            hsl = slice(hd * HEAD_PAD, (hd + 1) * HEAD_PAD)
            scores.append(_dot(k_ref[0, j, :, hsl], qt_ref[0, hsl, :]))
        for hd in range(MLA_HEADS):
            vsl = slice(hd * V_DIM, (hd + 1) * V_DIM)
            m_old, l_old = stats[hd]
            s = scores[hd]
            if masked:
                s = jnp.where(causal, s, -jnp.inf)
            m_new = jnp.maximum(m_old, jnp.max(s, axis=0, keepdims=True))
            alpha = jnp.exp(m_old - m_new)
            p = jnp.exp(s - m_new)
            new_stats.append((m_new, alpha * l_old + jnp.sum(p, axis=0, keepdims=True)))
            acc_sc[hd] = alpha * acc_sc[hd] + _dot(vt_ref[0, j, vsl, :], p.astype(BF16))
        return tuple(new_stats)

    init = tuple((jnp.full((1, tq), -jnp.inf, F32), jnp.zeros((1, tq), F32)) for _ in range(MLA_HEADS))
    stats = lax.fori_loop(0, i, functools.partial(block, masked=False), init)
    stats = block(i, stats, True)

    for pair in range(MLA_HEADS // 2):
        halves = [acc_sc[hd] / stats[hd][1] for hd in (2 * pair, 2 * pair + 1)]
        o_ref[0, :, pair * LANES:(pair + 1) * LANES] = jnp.concatenate(halves, axis=0).T.astype(BF16)


def _attn_call(qt, k, vt, lp):
    bsz = qt.shape[0]
    nq = lp // ROW_TILE
    qk_w = MLA_HEADS * HEAD_PAD
    return pl.pallas_call(
        _attn_kernel,
        grid=(bsz, nq),
        in_specs=[pl.BlockSpec((1, qk_w, ROW_TILE), lambda b, i: (b, 0, i)),
                  pl.BlockSpec((1, nq, ROW_TILE, qk_w), lambda b, i: (b, 0, 0, 0)),
                  pl.BlockSpec((1, nq, MLA_W, ROW_TILE), lambda b, i: (b, 0, 0, 0))],
        out_specs=pl.BlockSpec((1, ROW_TILE, MLA_W), lambda b, i: (b, i, 0)),
        out_shape=jax.ShapeDtypeStruct((bsz, lp, MLA_W), BF16),
        scratch_shapes=[pltpu.VMEM((MLA_HEADS, V_DIM, ROW_TILE), F32)],
        compiler_params=pltpu.CompilerParams(dimension_semantics=("arbitrary", "arbitrary"),
                                             vmem_limit_bytes=VMEM_LIMIT),
        name="mla_attention",
    )(qt, k, vt)


def _post_kernel(x_ref, u_ref, o_ref, smg_ref, png_ref, wg_ref, gb_ref, wpool_ref, wmla_ref, wconf_ref,
                 wsc_ref, wo_ref, pog_ref, out_ref):
    x = x_ref[...]
    h = _rms(x, png_ref[...]).astype(BF16)
    u = u_ref[...]
    ub = (o_ref[...].astype(F32) * smg_ref[...].astype(F32)).astype(BF16)
    branch_in = (u[:, 0:POOL_W], ub, u[:, POOL_W:POOL_W + CONF_W], u[:, POOL_W + CONF_W:])
    branch_w = (wpool_ref, wmla_ref, wconf_ref, wsc_ref)
    m = None
    for br in range(N_BRANCH):
        csl = slice(br * D_MODEL, (br + 1) * D_MODEL)
        gate = jax.nn.sigmoid(_dot(h, wg_ref[:, csl]) + gb_ref[:, csl])
        y = gate * _dot(branch_in[br], branch_w[br][...])
        m = y if m is None else m + y
    mo = _dot(m.astype(BF16), wo_ref[...])
    out_ref[...] = x + _rms(mo, pog_ref[...])


def _post_call(x2, u2, o2, smg2, lw):
    rows = x2.shape[0]
    tm = 2 * ROW_TILE
    row_spec = lambda w: pl.BlockSpec((tm, w), lambda r: (r, 0))
    consts = [lw["png"], lw["wg"], lw["gb"], lw["wpool"], lw["wmla"], lw["wconf"], lw["wsc"], lw["wo"],
              lw["pog"]]
    const_specs = [pl.BlockSpec(c.shape, lambda r, n=len(c.shape): (0,) * n) for c in consts]
    return pl.pallas_call(
        _post_kernel,
        grid=(rows // tm,),
        in_specs=[row_spec(D_MODEL), row_spec(POOL_W + CONF_W + SC_W), row_spec(MLA_W), row_spec(MLA_W)]
        + const_specs,
        out_specs=row_spec(D_MODEL),
        out_shape=jax.ShapeDtypeStruct((rows, D_MODEL), F32),
        compiler_params=pltpu.CompilerParams(dimension_semantics=("arbitrary",),
                                             vmem_limit_bytes=VMEM_LIMIT),
        name="mixer_post",
    )(x2, u2, o2, smg2, *consts)


def _rope_tables(lp):
    inv = 1.0 / (ROPE_THETA ** (jnp.arange(0, QK_ROPE, 2, dtype=F32) / QK_ROPE))
    ang = jnp.arange(lp, dtype=F32)[:, None] * inv[None, :]
    cos, sin = jnp.cos(ang), jnp.sin(ang)
    zeros_tail = jnp.zeros((lp, HEAD_PAD - QK_NOPE - QK_ROPE), F32)
    c_rope = jnp.concatenate([cos, cos], axis=1)
    s_rope = jnp.concatenate([-sin, sin], axis=1)
    scale = (QK_NOPE + QK_ROPE) ** -0.5
    cq = jnp.concatenate([jnp.ones((lp, QK_NOPE), F32), c_rope, zeros_tail], axis=1) * scale
    sq = jnp.concatenate([jnp.zeros((lp, QK_NOPE), F32), s_rope, zeros_tail], axis=1) * scale
    ck = jnp.concatenate([jnp.zeros((lp, QK_NOPE), F32), c_rope, zeros_tail], axis=1)
    sk = jnp.concatenate([jnp.zeros((lp, QK_NOPE), F32), s_rope, zeros_tail], axis=1)
    return cq.T, sq.T, ck, sk


def _swap_halves(w):
    half = QK_ROPE // 2
    return jnp.concatenate([w[..., half:], w[..., :half]], axis=-1)


def _layer_weights(i, w_in, p):
    idx = [int(s) for s in np.cumsum(IN_SPLITS)[:-1]]
    (w_pv, w_pg, w_cq, w_ckv, w_kr, w_mg, w_cu, w_cg, w_sb, w_sg, w_gl) = jnp.split(w_in[i], idx, axis=-1)
    zpad = lambda n: jnp.zeros((D_MODEL, n), F32)
    tail = HEAD_PAD - QK_NOPE - QK_ROPE
    kr_tile = jnp.concatenate([zpad(QK_NOPE), w_kr, zpad(tail)], axis=1)
    krs_tile = jnp.concatenate([zpad(QK_NOPE), _swap_halves(w_kr), zpad(tail)], axis=1)
    w1 = jnp.concatenate([w_pv, w_pg, w_cq, w_ckv, kr_tile, krs_tile, w_mg, w_cu, w_cg, w_sb, w_sg],
                         axis=1).astype(BF16)

    pbd = jnp.zeros((POOL_W, POOL_W), F32)
    for g in range(len(POOL_WINDOWS)):
        sl = slice(g * POOL_GROUP, (g + 1) * POOL_GROUP)
        pbd = pbd.at[sl, sl].set(p["pool_w"][i, g])

    wuq = p["w_uq"][i].reshape(Q_RANK, MLA_HEADS, QK_NOPE + QK_ROPE)
    zq = lambda n: jnp.zeros((Q_RANK, MLA_HEADS, n), F32)
    wq = jnp.concatenate([wuq, zq(tail)], axis=-1).reshape(Q_RANK, MLA_HEADS * HEAD_PAD)
    wqs = jnp.concatenate([zq(QK_NOPE), _swap_halves(wuq[..., QK_NOPE:]), zq(tail)],
                          axis=-1).reshape(Q_RANK, MLA_HEADS * HEAD_PAD)
    wukv = p["w_ukv"][i].reshape(KV_RANK, MLA_HEADS, QK_NOPE + V_DIM)
    wk = jnp.concatenate([wukv[..., :QK_NOPE], jnp.zeros((KV_RANK, MLA_HEADS, HEAD_PAD - QK_NOPE), F32)],
                         axis=-1).reshape(KV_RANK, MLA_HEADS * HEAD_PAD)
    wv = wukv[..., QK_NOPE:].reshape(KV_RANK, MLA_W)
    row = lambda a: a[i][None, :]
    return dict(
        png=row(p["pre_norm_g"]), w1=w1, pbd=pbd.astype(BF16), psc=row(p["pool_scale"]),
        qng=row(p["q_norm_g"]), wq=wq.T.astype(BF16), wqs=wqs.T.astype(BF16), kvg=row(p["kv_norm_g"]),
        wk=wk.astype(BF16), wv=wv.T.astype(BF16), cw=p["conf_dw_w"][i], cb=row(p["conf_dw_b"]),
        lg=row(p["conf_ln_g"]), lb=row(p["conf_ln_b"]), sw=p["sc_dw_w"][i],
        wg=w_gl.astype(BF16), gb=row(p["gate_bias"]), wpool=p["w_out_pool"][i].astype(BF16),
        wmla=p["w_out_mla"][i].astype(BF16), wconf=p["w_out_conf"][i].astype(BF16),
        wsc=p["w_out_sc"][i].astype(BF16), wo=p["w_o"][i].astype(BF16), pog=row(p["post_norm_g"]))


def kernel(x, meta_tokens, pre_norm_g, w_in, gate_bias, pool_w, pool_scale, w_out_pool, q_norm_g, w_uq,
           kv_norm_g, w_ukv, w_out_mla, conf_dw_w, conf_dw_b, conf_ln_g, conf_ln_b, w_out_conf, sc_dw_w,
           w_out_sc, w_o, post_norm_g):
    bsz, seq, _ = x.shape
    l_real = N_META + seq
    lp = pl.cdiv(l_real, ROW_TILE) * ROW_TILE
    assert (bsz * lp) % (2 * ROW_TILE) == 0
    params = dict(pre_norm_g=pre_norm_g, gate_bias=gate_bias, pool_w=pool_w, pool_scale=pool_scale,
                  w_out_pool=w_out_pool, q_norm_g=q_norm_g, w_uq=w_uq, kv_norm_g=kv_norm_g, w_ukv=w_ukv,
                  w_out_mla=w_out_mla, conf_dw_w=conf_dw_w, conf_dw_b=conf_dw_b, conf_ln_g=conf_ln_g,
                  conf_ln_b=conf_ln_b, w_out_conf=w_out_conf, sc_dw_w=sc_dw_w, w_out_sc=w_out_sc, w_o=w_o,
                  post_norm_g=post_norm_g)
    meta = jnp.broadcast_to(meta_tokens[None].astype(x.dtype), (bsz, N_META, D_MODEL))
    h_res = jnp.concatenate([meta, x, jnp.zeros((bsz, lp - l_real, D_MODEL), x.dtype)], axis=1)
    tabs = _rope_tables(lp)

    for i in range(DEPTH):
        lw = _layer_weights(i, w_in, params)
        q, k, v, smg, u = _pre_call(h_res, lw, tabs, lp)
        o = _attn_call(q, k, v, lp)
        flat = lambda a: a.reshape(bsz * lp, a.shape[-1])
        h_res = _post_call(flat(h_res), flat(u), flat(o), flat(smg), lw).reshape(bsz, lp, D_MODEL)

    return h_res[:, N_META:l_real]
```

```python
import functools

import jax
import jax.numpy as jnp
import numpy as np
from jax import lax
from jax.experimental import pallas as pl
from jax.experimental.pallas import tpu as pltpu

F32 = jnp.float32
BF16 = jnp.bfloat16

D_MODEL = 1024
DEPTH = 4
N_META = 16
EPS = 1e-6
N_BRANCH = 4
POOL_WINDOWS = (2, 4, 8, 16)
POOL_GROUP = 64
POOL_W = 256
MLA_HEADS = 8
QK_NOPE = 64
QK_ROPE = 32
V_DIM = 64
Q_RANK = 256
KV_RANK = 128
ROPE_THETA = 10000.0
MLA_W = MLA_HEADS * V_DIM
CONF_W = 256
CONF_K = 31
SC_W = 256
SC_K = 3
IN_SPLITS = (POOL_W, POOL_W, Q_RANK, KV_RANK, QK_ROPE, MLA_W, 2 * CONF_W, CONF_W, 3 * SC_W, SC_W,
             N_BRANCH * D_MODEL)

LANES = 128
SUBLANES = 8
HEAD_PAD = LANES
ROW_TILE = 256
POOL_HALO = 16
CONF_HALO = 32
SC_HALO = 8
SUM_ROWS = 16
LOG2_E = 1.4426950408889634
VMEM_LIMIT = 56 * 1024 * 1024

C_PV, C_PG, C_CQ, C_CKV, C_KR, C_KRS, C_MG, C_CU, C_CG, C_SB, C_SG, C_END = (
    0, 256, 512, 768, 896, 1024, 1152, 1664, 2176, 2432, 3200, 3456)


def _rms(x, g):
    return x * lax.rsqrt(jnp.mean(x * x, axis=-1, keepdims=True) + EPS) * g


def _dot(a, b):
    return jnp.dot(a, b, preferred_element_type=F32)


def _dot_nt(a, b):
    return lax.dot_general(a, b, (((1,), (1,)), ((), ())), preferred_element_type=F32)


def _shifted_rows(buf, shifted, off, rows, lanes):
    tile, sh = divmod(off, SUBLANES)
    if sh == 0:
        return buf[off:off + rows, lanes]
    return shifted[sh - 1, tile * SUBLANES:tile * SUBLANES + rows, lanes]


def _pre_kernel(x_ref, png_ref, w1_ref, pbd_ref, psc_ref, qng_ref, wq_ref, wqs_ref, kvg_ref, wk_ref,
                wv_ref, cq_ref, sq_ref, ck_ref, sk_ref, cw_ref, cb_ref, lg_ref, lb_ref, sw_ref,
                q_ref, k_ref, v_ref, smg_ref, u_ref, pbuf, cbuf, sbuf, pshift, cshift):
    tm = ROW_TILE
    t = pl.program_id(1)

    @pl.when(t == 0)
    def _():
        pbuf[0:POOL_HALO, :] = jnp.zeros((POOL_HALO, POOL_W), F32)
        cbuf[0:CONF_HALO, :] = jnp.zeros((CONF_HALO, CONF_W), F32)
        sbuf[0:SC_HALO, :] = jnp.zeros((SC_HALO, SC_W), F32)

    h = _rms(x_ref[0], png_ref[...]).astype(BF16)

    def proj(lo, hi):
        return _dot(h, w1_ref[:, lo:hi])

    zu = proj(C_CU, C_SB)
    glu = zu[:, 0:CONF_W] * jax.nn.sigmoid(zu[:, CONF_W:2 * CONF_W])
    cbuf[CONF_HALO:CONF_HALO + tm, :] = glu
    for sh in range(1, SUBLANES):
        cshift[sh - 1] = cbuf[sh:sh + CONF_HALO + tm - SUBLANES, :]

    def conv_taps(acc, k_lo, k_hi):
        base = CONF_HALO - (CONF_K - 1)
        for kk in range(k_lo, k_hi):
            acc = acc + cw_ref[kk:kk + 1, :] * _shifted_rows(cbuf, cshift, base + kk, tm, slice(None))
        return acc

    tap_cuts = (0, 8, 16, 24, CONF_K)
    acc = conv_taps(jnp.zeros((tm, CONF_W), F32) + cb_ref[...], tap_cuts[0], tap_cuts[1])
    zp = proj(C_PV, C_CQ)
    acc = conv_taps(acc, tap_cuts[1], tap_cuts[2])
    zc = proj(C_CQ, C_MG)
    cqn = _rms(zc[:, 0:Q_RANK], qng_ref[...]).astype(BF16)
    ckvn = _rms(zc[:, Q_RANK:Q_RANK + KV_RANK], kvg_ref[...]).astype(BF16)
    qa = _dot_nt(wq_ref[...], cqn)
    qb = _dot_nt(wqs_ref[...], cqn)
    kn = _dot(ckvn, wk_ref[...])
    vt = _dot_nt(wv_ref[...], ckvn)
    acc = conv_taps(acc, tap_cuts[2], tap_cuts[3])
    zm = proj(C_MG, C_CU)
    zs = proj(C_SB, C_END)
    acc = conv_taps(acc, tap_cuts[3], tap_cuts[4])
    mu = jnp.mean(acc, axis=-1, keepdims=True)
    cen = acc - mu
    var = jnp.mean(cen * cen, axis=-1, keepdims=True)
    yc = cen * lax.rsqrt(var + EPS) * lg_ref[...] + lb_ref[...]
    yc = jax.nn.silu(yc) * jax.nn.silu(zu[:, 2 * CONF_W:3 * CONF_W])
    u_ref[0, :, POOL_W:POOL_W + CONF_W] = yc.astype(BF16)
    cbuf[0:CONF_HALO, :] = cbuf[tm:tm + CONF_HALO, :]

    pv = zp[:, :POOL_W]
    pbuf[POOL_HALO:POOL_HALO + tm, :] = pv
    pos = lax.broadcasted_iota(jnp.int32, (tm, LANES), 0) + t * tm
    low_half = lax.broadcasted_iota(jnp.int32, (tm, LANES), 1) < POOL_GROUP

    def window_sum(col, acc, j_lo, j_hi):
        for j in range(j_lo, j_hi):
            acc = acc + _shifted_rows(pbuf, pshift, POOL_HALO - j, tm, slice(col, col + LANES))
        return acc

    pooled = []
    for tile, (w_lo, w_hi) in enumerate(((POOL_WINDOWS[0], POOL_WINDOWS[1]),
                                         (POOL_WINDOWS[2], POOL_WINDOWS[3]))):
        col = tile * LANES
        lanes = slice(col, col + LANES)
        for sh in sorted({(POOL_HALO - j) % SUBLANES for j in range(1, w_hi)} - {0}):
            pshift[sh - 1, :, lanes] = pbuf[sh:sh + POOL_HALO + tm - SUBLANES, lanes]
        self_v = pv[:, col:col + LANES]
        s_lo = window_sum(col, self_v, 1, w_lo)
        s_hi = window_sum(col, s_lo, w_lo, w_hi)
        width = jnp.where(low_half, w_lo, w_hi)
        cnt = jnp.minimum(pos + 1, width).astype(F32)
        pooled.append(jnp.where(low_half, s_lo, s_hi) / cnt - self_v)
    p = jnp.concatenate(pooled, axis=1).astype(BF16)
    ya = _dot(p, pbd_ref[...]) * psc_ref[...] * jax.nn.silu(zp[:, POOL_W:])
    u_ref[0, :, 0:POOL_W] = ya.astype(BF16)
    pbuf[0:POOL_HALO, :] = pbuf[tm:tm + POOL_HALO, :]

    cq_t = cq_ref[...]
    sq_t = sq_ref[...]
    for hd in range(MLA_HEADS):
        sl = slice(hd * HEAD_PAD, (hd + 1) * HEAD_PAD)
        q_ref[0, sl, :] = (qa[sl, :] * cq_t + qb[sl, :] * sq_t).astype(BF16)
    k_rope = (zc[:, C_KR - C_CQ:C_KRS - C_CQ] * ck_ref[...]
              + zc[:, C_KRS - C_CQ:C_MG - C_CQ] * sk_ref[...])
    for hd in range(MLA_HEADS):
        sl = slice(hd * HEAD_PAD, (hd + 1) * HEAD_PAD)
        k_ref[0, 0, :, sl] = (kn[:, sl] + k_rope).astype(BF16)
    v_ref[0, 0] = vt.astype(BF16)
    smg_ref[0] = jax.nn.silu(zm).astype(BF16)

    sbuf[SC_HALO:SC_HALO + tm, :] = zs[:, SC_W:2 * SC_W] * zs[:, 2 * SC_W:3 * SC_W]
    conv = jnp.zeros((tm, SC_W), F32)
    base = SC_HALO - (SC_K - 1)
    for kk in range(SC_K):
        conv = conv + sw_ref[kk:kk + 1, :] * sbuf[base + kk:base + kk + tm, :]
    yd = zs[:, 0:SC_W] * conv * jax.nn.silu(zs[:, 3 * SC_W:4 * SC_W])
    u_ref[0, :, POOL_W + CONF_W:POOL_W + CONF_W + SC_W] = yd.astype(BF16)
    sbuf[0:SC_HALO, :] = sbuf[tm:tm + SC_HALO, :]


def _const_spec(shape):
    return pl.BlockSpec(shape, lambda b, t: (0,) * len(shape))


def _pre_call(x, lw, tabs, lp):
    bsz = x.shape[0]
    nt = lp // ROW_TILE
    row_spec = lambda w: pl.BlockSpec((1, ROW_TILE, w), lambda b, t: (b, t, 0))
    tab_spec = pl.BlockSpec((ROW_TILE, LANES), lambda b, t: (t, 0))
    tab_t_spec = pl.BlockSpec((LANES, ROW_TILE), lambda b, t: (0, t))
    consts = [lw["png"], lw["w1"], lw["pbd"], lw["psc"], lw["qng"], lw["wq"], lw["wqs"], lw["kvg"],
              lw["wk"], lw["wv"]]
    consts2 = [lw["cw"], lw["cb"], lw["lg"], lw["lb"], lw["sw"]]
    in_specs = ([row_spec(D_MODEL)] + [_const_spec(c.shape) for c in consts]
                + [tab_t_spec, tab_t_spec, tab_spec, tab_spec] + [_const_spec(c.shape) for c in consts2])
    qk_w = MLA_HEADS * HEAD_PAD
    u_w = POOL_W + CONF_W + SC_W
    out_specs = [pl.BlockSpec((1, qk_w, ROW_TILE), lambda b, t: (b, 0, t)),
                 pl.BlockSpec((1, 1, ROW_TILE, qk_w), lambda b, t: (b, t, 0, 0)),
                 pl.BlockSpec((1, 1, MLA_W, ROW_TILE), lambda b, t: (b, t, 0, 0)),
                 row_spec(MLA_W), row_spec(u_w)]
    out_shape = [jax.ShapeDtypeStruct((bsz, qk_w, lp), BF16),
                 jax.ShapeDtypeStruct((bsz, nt, ROW_TILE, qk_w), BF16),
                 jax.ShapeDtypeStruct((bsz, nt, MLA_W, ROW_TILE), BF16),
                 jax.ShapeDtypeStruct((bsz, lp, MLA_W), BF16),
                 jax.ShapeDtypeStruct((bsz, lp, u_w), BF16)]
    return pl.pallas_call(
        _pre_kernel,
        grid=(bsz, nt),
        in_specs=in_specs,
        out_specs=out_specs,
        out_shape=out_shape,
        scratch_shapes=[pltpu.VMEM((POOL_HALO + ROW_TILE, POOL_W), F32),
                        pltpu.VMEM((CONF_HALO + ROW_TILE, CONF_W), F32),
                        pltpu.VMEM((SC_HALO + ROW_TILE, SC_W), F32),
                        pltpu.VMEM((SUBLANES - 1, POOL_HALO + ROW_TILE - SUBLANES, POOL_W), F32),
                        pltpu.VMEM((SUBLANES - 1, CONF_HALO + ROW_TILE - SUBLANES, CONF_W), F32)],
        compiler_params=pltpu.CompilerParams(dimension_semantics=("arbitrary", "arbitrary"),
                                             vmem_limit_bytes=VMEM_LIMIT),
        name="mixer_pre",
    )(x, *consts, *tabs, *consts2)


def _attn_kernel(qt_ref, k_ref, vt_ref, o_ref, acc_sc, s_even, s_odd):
    tq = ROW_TILE
    i = pl.program_id(1)
    key_idx = lax.broadcasted_iota(jnp.int32, (tq, tq), 0)
    qry_idx = lax.broadcasted_iota(jnp.int32, (tq, tq), 1)
    causal = key_idx <= qry_idx
    ones_rows = jnp.ones((SUM_ROWS, tq), BF16)

    acc_sc[...] = jnp.zeros(acc_sc.shape, F32)

    def scores_into(s_ref, j):
        for hd in range(MLA_HEADS):
            hsl = slice(hd * HEAD_PAD, (hd + 1) * HEAD_PAD)
            s_ref[hd] = _dot(k_ref[0, j, :, hsl], qt_ref[0, hsl, :])

    def consume(s_ref, j, maxes, masked):
        new_maxes = []
        for hd in range(MLA_HEADS):
            vsl = slice(hd * V_DIM, (hd + 1) * V_DIM)
            s = s_ref[hd]
            if masked:
                s = jnp.where(causal, s, -jnp.inf)
            m_new = jnp.maximum(maxes[hd], jnp.max(s, axis=0, keepdims=True))
            alpha = jnp.exp2(maxes[hd] - m_new)
            p = jnp.exp2(s - m_new).astype(BF16)
            v_ext = jnp.concatenate([vt_ref[0, j, vsl, :], ones_rows], axis=0)
            acc_sc[hd] = alpha * acc_sc[hd] + _dot(v_ext, p)
            new_maxes.append(m_new)
        return tuple(new_maxes)

    def finish():
        for pair in range(MLA_HEADS // 2):
            halves = []
            for hd in (2 * pair, 2 * pair + 1):
                acc = acc_sc[hd]
                halves.append(acc[0:V_DIM, :] / acc[V_DIM:V_DIM + 1, :])
            o_ref[0, :, pair * LANES:(pair + 1) * LANES] = jnp.concatenate(halves, axis=0).T.astype(BF16)

    scores_into(s_even, 0)

    def two_blocks(jj, maxes):
        j = 2 * jj
        scores_into(s_odd, j + 1)
        maxes = consume(s_even, j, maxes, False)
        scores_into(s_even, j + 2)
        return consume(s_odd, j + 1, maxes, False)

    init = tuple(jnp.full((1, tq), -jnp.inf, F32) for _ in range(MLA_HEADS))
    maxes = lax.fori_loop(0, i // 2, two_blocks, init)

    @pl.when(i % 2 == 0)
    def _():
        consume(s_even, i, maxes, True)
        finish()

    @pl.when(i % 2 == 1)
    def _():
        scores_into(s_odd, i)
        consume(s_odd, i, consume(s_even, i - 1, maxes, False), True)
        finish()


def _attn_call(qt, k, vt, lp):
    bsz = qt.shape[0]
    nq = lp // ROW_TILE
    qk_w = MLA_HEADS * HEAD_PAD
    return pl.pallas_call(
        _attn_kernel,
        grid=(bsz, nq),
        in_specs=[pl.BlockSpec((1, qk_w, ROW_TILE), lambda b, i: (b, 0, i)),
                  pl.BlockSpec((1, nq, ROW_TILE, qk_w), lambda b, i: (b, 0, 0, 0)),
                  pl.BlockSpec((1, nq, MLA_W, ROW_TILE), lambda b, i: (b, 0, 0, 0))],
        out_specs=pl.BlockSpec((1, ROW_TILE, MLA_W), lambda b, i: (b, i, 0)),
        out_shape=jax.ShapeDtypeStruct((bsz, lp, MLA_W), BF16),
        scratch_shapes=[pltpu.VMEM((MLA_HEADS, V_DIM + SUM_ROWS, ROW_TILE), F32),
                        pltpu.VMEM((MLA_HEADS, ROW_TILE, ROW_TILE), F32),
                        pltpu.VMEM((MLA_HEADS, ROW_TILE, ROW_TILE), F32)],
        compiler_params=pltpu.CompilerParams(dimension_semantics=("arbitrary", "arbitrary"),
                                             vmem_limit_bytes=VMEM_LIMIT),
        name="mla_attention",
    )(qt, k, vt)


def _post_kernel(x_ref, u_ref, o_ref, smg_ref, png_ref, wg_ref, gb_ref, wpool_ref, wmla_ref, wconf_ref,
                 wsc_ref, wo_ref, pog_ref, out_ref):
    x = x_ref[...]
    h = _rms(x, png_ref[...]).astype(BF16)
    u = u_ref[...]
    ub = (o_ref[...].astype(F32) * smg_ref[...].astype(F32)).astype(BF16)
    branch_in = (u[:, 0:POOL_W], ub, u[:, POOL_W:POOL_W + CONF_W], u[:, POOL_W + CONF_W:])
    branch_w = (wpool_ref, wmla_ref, wconf_ref, wsc_ref)
    m = None
    for br in range(N_BRANCH):
        csl = slice(br * D_MODEL, (br + 1) * D_MODEL)
        gate = jax.nn.sigmoid(_dot(h, wg_ref[:, csl]) + gb_ref[:, csl])
        y = gate * _dot(branch_in[br], branch_w[br][...])
        m = y if m is None else m + y
    mo = _dot(m.astype(BF16), wo_ref[...])
    out_ref[...] = x + _rms(mo, pog_ref[...])


def _post_call(x2, u2, o2, smg2, lw):
    rows = x2.shape[0]
    tm = 2 * ROW_TILE
    row_spec = lambda w: pl.BlockSpec((tm, w), lambda r: (r, 0))
    consts = [lw["png"], lw["wg"], lw["gb"], lw["wpool"], lw["wmla"], lw["wconf"], lw["wsc"], lw["wo"],
              lw["pog"]]
    const_specs = [pl.BlockSpec(c.shape, lambda r, n=len(c.shape): (0,) * n) for c in consts]
    return pl.pallas_call(
        _post_kernel,
        grid=(rows // tm,),
        in_specs=[row_spec(D_MODEL), row_spec(POOL_W + CONF_W + SC_W), row_spec(MLA_W), row_spec(MLA_W)]
        + const_specs,
        out_specs=row_spec(D_MODEL),
        out_shape=jax.ShapeDtypeStruct((rows, D_MODEL), F32),
        compiler_params=pltpu.CompilerParams(dimension_semantics=("arbitrary",),
                                             vmem_limit_bytes=VMEM_LIMIT),
        name="mixer_post",
    )(x2, u2, o2, smg2, *consts)


def _rope_tables(lp):
    inv = 1.0 / (ROPE_THETA ** (jnp.arange(0, QK_ROPE, 2, dtype=F32) / QK_ROPE))
    ang = jnp.arange(lp, dtype=F32)[:, None] * inv[None, :]
    cos, sin = jnp.cos(ang), jnp.sin(ang)
    zeros_tail = jnp.zeros((lp, HEAD_PAD - QK_NOPE - QK_ROPE), F32)
    c_rope = jnp.concatenate([cos, cos], axis=1)
    s_rope = jnp.concatenate([-sin, sin], axis=1)
    scale = (QK_NOPE + QK_ROPE) ** -0.5 * LOG2_E
    cq = jnp.concatenate([jnp.ones((lp, QK_NOPE), F32), c_rope, zeros_tail], axis=1) * scale
    sq = jnp.concatenate([jnp.zeros((lp, QK_NOPE), F32), s_rope, zeros_tail], axis=1) * scale
    ck = jnp.concatenate([jnp.zeros((lp, QK_NOPE), F32), c_rope, zeros_tail], axis=1)
    sk = jnp.concatenate([jnp.zeros((lp, QK_NOPE), F32), s_rope, zeros_tail], axis=1)
    return cq.T, sq.T, ck, sk


def _swap_halves(w):
    half = QK_ROPE // 2
    return jnp.concatenate([w[..., half:], w[..., :half]], axis=-1)


def _layer_weights(i, w_in, p):
    idx = [int(s) for s in np.cumsum(IN_SPLITS)[:-1]]
    (w_pv, w_pg, w_cq, w_ckv, w_kr, w_mg, w_cu, w_cg, w_sb, w_sg, w_gl) = jnp.split(w_in[i], idx, axis=-1)
    zpad = lambda n: jnp.zeros((D_MODEL, n), F32)
    tail = HEAD_PAD - QK_NOPE - QK_ROPE
    kr_tile = jnp.concatenate([zpad(QK_NOPE), w_kr, zpad(tail)], axis=1)
    krs_tile = jnp.concatenate([zpad(QK_NOPE), _swap_halves(w_kr), zpad(tail)], axis=1)
    w1 = jnp.concatenate([w_pv, w_pg, w_cq, w_ckv, kr_tile, krs_tile, w_mg, w_cu, w_cg, w_sb, w_sg],
                         axis=1).astype(BF16)

    pbd = jnp.zeros((POOL_W, POOL_W), F32)
    for g in range(len(POOL_WINDOWS)):
        sl = slice(g * POOL_GROUP, (g + 1) * POOL_GROUP)
        pbd = pbd.at[sl, sl].set(p["pool_w"][i, g])

    wuq = p["w_uq"][i].reshape(Q_RANK, MLA_HEADS, QK_NOPE + QK_ROPE)
    zq = lambda n: jnp.zeros((Q_RANK, MLA_HEADS, n), F32)
    wq = jnp.concatenate([wuq, zq(tail)], axis=-1).reshape(Q_RANK, MLA_HEADS * HEAD_PAD)
    wqs = jnp.concatenate([zq(QK_NOPE), _swap_halves(wuq[..., QK_NOPE:]), zq(tail)],
                          axis=-1).reshape(Q_RANK, MLA_HEADS * HEAD_PAD)
    wukv = p["w_ukv"][i].reshape(KV_RANK, MLA_HEADS, QK_NOPE + V_DIM)
    wk = jnp.concatenate([wukv[..., :QK_NOPE], jnp.zeros((KV_RANK, MLA_HEADS, HEAD_PAD - QK_NOPE), F32)],
                         axis=-1).reshape(KV_RANK, MLA_HEADS * HEAD_PAD)
    wv = wukv[..., QK_NOPE:].reshape(KV_RANK, MLA_W)
    row = lambda a: a[i][None, :]
    return dict(
        png=row(p["pre_norm_g"]), w1=w1, pbd=pbd.astype(BF16), psc=row(p["pool_scale"]),
        qng=row(p["q_norm_g"]), wq=wq.T.astype(BF16), wqs=wqs.T.astype(BF16), kvg=row(p["kv_norm_g"]),
        wk=wk.astype(BF16), wv=wv.T.astype(BF16), cw=p["conf_dw_w"][i], cb=row(p["conf_dw_b"]),
        lg=row(p["conf_ln_g"]), lb=row(p["conf_ln_b"]), sw=p["sc_dw_w"][i],
        wg=w_gl.astype(BF16), gb=row(p["gate_bias"]), wpool=p["w_out_pool"][i].astype(BF16),
        wmla=p["w_out_mla"][i].astype(BF16), wconf=p["w_out_conf"][i].astype(BF16),
        wsc=p["w_out_sc"][i].astype(BF16), wo=p["w_o"][i].astype(BF16), pog=row(p["post_norm_g"]))


def kernel(x, meta_tokens, pre_norm_g, w_in, gate_bias, pool_w, pool_scale, w_out_pool, q_norm_g, w_uq,
           kv_norm_g, w_ukv, w_out_mla, conf_dw_w, conf_dw_b, conf_ln_g, conf_ln_b, w_out_conf, sc_dw_w,
           w_out_sc, w_o, post_norm_g):
    bsz, seq, _ = x.shape
    l_real = N_META + seq
    lp = pl.cdiv(l_real, ROW_TILE) * ROW_TILE
    assert (bsz * lp) % (2 * ROW_TILE) == 0
    params = dict(pre_norm_g=pre_norm_g, gate_bias=gate_bias, pool_w=pool_w, pool_scale=pool_scale,
                  w_out_pool=w_out_pool, q_norm_g=q_norm_g, w_uq=w_uq, kv_norm_g=kv_norm_g, w_ukv=w_ukv,
                  w_out_mla=w_out_mla, conf_dw_w=conf_dw_w, conf_dw_b=conf_dw_b, conf_ln_g=conf_ln_g,
                  conf_ln_b=conf_ln_b, w_out_conf=w_out_conf, sc_dw_w=sc_dw_w, w_out_sc=w_out_sc, w_o=w_o,
                  post_norm_g=post_norm_g)
    meta = jnp.broadcast_to(meta_tokens[None].astype(x.dtype), (bsz, N_META, D_MODEL))
    h_res = jnp.concatenate([meta, x, jnp.zeros((bsz, lp - l_real, D_MODEL), x.dtype)], axis=1)
    tabs = _rope_tables(lp)

    for i in range(DEPTH):
        lw = _layer_weights(i, w_in, params)
        q, k, v, smg, u = _pre_call(h_res, lw, tabs, lp)
        o = _attn_call(q, k, v, lp)
        flat = lambda a: a.reshape(bsz * lp, a.shape[-1])
        h_res = _post_call(flat(h_res), flat(u), flat(o), flat(smg), lw).reshape(bsz, lp, D_MODEL)

    return h_res[:, N_META:l_real]
```

```python
import functools

import jax
import jax.numpy as jnp
import numpy as np
from jax import lax
from jax.experimental import pallas as pl
from jax.experimental.pallas import tpu as pltpu

F32 = jnp.float32
BF16 = jnp.bfloat16

D_MODEL = 1024
DEPTH = 4
N_META = 16
EPS = 1e-6
N_BRANCH = 4
POOL_WINDOWS = (2, 4, 8, 16)
POOL_GROUP = 64
POOL_W = 256
MLA_HEADS = 8
QK_NOPE = 64
QK_ROPE = 32
QK_DIM = QK_NOPE + QK_ROPE
V_DIM = 64
Q_RANK = 256
KV_RANK = 128
ROPE_THETA = 10000.0
MLA_W = MLA_HEADS * V_DIM
CONF_W = 256
CONF_K = 31
SC_W = 256
SC_K = 3
IN_SPLITS = (POOL_W, POOL_W, Q_RANK, KV_RANK, QK_ROPE, MLA_W, 2 * CONF_W, CONF_W, 3 * SC_W, SC_W,
             N_BRANCH * D_MODEL)
U_W = POOL_W + CONF_W + SC_W

LANES = 128
SUBLANES = 8
HEAD_PAD = LANES
QK_W = MLA_HEADS * HEAD_PAD
ROW_TILE = 256
POST_TILE = 2 * ROW_TILE
POOL_HALO = 16
CONF_HALO = 32
SC_HALO = 8
SUM_ROWS = 16
META_KEYS = LANES
VMEM_LIMIT = 56 * 1024 * 1024
LOG2_E = 1.4426950408889634
Q_SCALE = QK_DIM ** -0.5 * LOG2_E

C_PV, C_PG, C_CQ, C_CKV, C_KR, C_MG, C_CU, C_CG, C_SB, C_SG, C_END = (
    0, 256, 512, 768, 896, 1024, 1536, 2048, 2304, 3072, 3328)


def _rms(x, g):
    return x * lax.rsqrt(jnp.mean(x * x, axis=-1, keepdims=True) + EPS) * g


def _dot(a, b):
    return jnp.dot(a, b, preferred_element_type=F32)


def _dot_nt(a, b):
    return lax.dot_general(a, b, (((1,), (1,)), ((), ())), preferred_element_type=F32)


def _shifted_rows(buf, shifted, off, rows, lanes):
    tile, sh = divmod(off, SUBLANES)
    if sh == 0:
        return buf[off:off + rows, lanes]
    return shifted[sh - 1, tile * SUBLANES:tile * SUBLANES + rows, lanes]


def _pre_kernel(copy_x, x_ref, xm_ref, png_ref, w1_ref, pbd_ref, psc_ref, qng_ref, wq_ref, kvg_ref, wk_ref,
                wv_ref, qc_ref, qs_ref, ck_ref, sk_ref, cw_ref, cb_ref, lg_ref, lb_ref, sw_ref, *rest):
    if copy_x:
        q_ref, k_ref, v_ref, smg_ref, u_ref, xo_ref, pbuf, cbuf, sbuf, pshift, cshift = rest
    else:
        q_ref, k_ref, v_ref, smg_ref, u_ref, pbuf, cbuf, sbuf, pshift, cshift = rest
    tm = ROW_TILE
    t = pl.program_id(1)
    is_meta = t == 0

    @pl.when(is_meta)
    def _():
        pbuf[0:POOL_HALO, :] = jnp.zeros((POOL_HALO, POOL_W), F32)
        cbuf[0:CONF_HALO, :] = jnp.zeros((CONF_HALO, CONF_W), F32)
        sbuf[0:SC_HALO, :] = jnp.zeros((SC_HALO, SC_W), F32)

    x = jnp.where(is_meta, xm_ref[0], x_ref[0])
    if copy_x:
        xo_ref[0] = x
    h = _rms(x, png_ref[...]).astype(BF16)
    valid = pl.multiple_of(jnp.where(is_meta, N_META, tm), SUBLANES)

    def proj(lo, hi):
        return _dot(h, w1_ref[:, lo:hi])

    zu = proj(C_CU, C_SB)
    glu = zu[:, 0:CONF_W] * jax.nn.sigmoid(zu[:, CONF_W:2 * CONF_W])
    cbuf[CONF_HALO:CONF_HALO + tm, :] = glu
    for sh in range(1, SUBLANES):
        cshift[sh - 1] = cbuf[sh:sh + CONF_HALO + tm - SUBLANES, :]

    def conv_taps(acc, k_lo, k_hi):
        base = CONF_HALO - (CONF_K - 1)
        for kk in range(k_lo, k_hi):
            acc = acc + cw_ref[kk:kk + 1, :] * _shifted_rows(cbuf, cshift, base + kk, tm, slice(None))
        return acc

    tap_cuts = (0, 8, 16, 24, CONF_K)
    acc = conv_taps(jnp.zeros((tm, CONF_W), F32) + cb_ref[...], tap_cuts[0], tap_cuts[1])
    zp = proj(C_PV, C_CQ)
    acc = conv_taps(acc, tap_cuts[1], tap_cuts[2])
    zc = proj(C_CQ, C_MG)
    cqn = _rms(zc[:, 0:Q_RANK], qng_ref[...]).astype(BF16)
    ckvn = _rms(zc[:, Q_RANK:Q_RANK + KV_RANK], kvg_ref[...]).astype(BF16)
    qa = _dot_nt(wq_ref[...], cqn)
    kn = _dot(ckvn, wk_ref[...])
    vt = _dot_nt(wv_ref[...], ckvn)
    acc = conv_taps(acc, tap_cuts[2], tap_cuts[3])
    zm = proj(C_MG, C_CU)
    zs = proj(C_SB, C_END)
    acc = conv_taps(acc, tap_cuts[3], tap_cuts[4])
    mu = jnp.mean(acc, axis=-1, keepdims=True)
    cen = acc - mu
    var = jnp.mean(cen * cen, axis=-1, keepdims=True)
    yc = cen * lax.rsqrt(var + EPS) * lg_ref[...] + lb_ref[...]
    yc = jax.nn.silu(yc) * jax.nn.silu(zu[:, 2 * CONF_W:3 * CONF_W])
    u_ref[0, :, POOL_W:POOL_W + CONF_W] = yc.astype(BF16)
    cbuf[0:CONF_HALO, :] = cbuf[pl.ds(valid, CONF_HALO), :]

    pv = zp[:, :POOL_W]
    pbuf[POOL_HALO:POOL_HALO + tm, :] = pv
    first_pos = jnp.where(is_meta, 0, N_META + (t - 1) * tm)
    pos = lax.broadcasted_iota(jnp.int32, (tm, LANES), 0) + first_pos
    low_half = lax.broadcasted_iota(jnp.int32, (tm, LANES), 1) < POOL_GROUP

    def window_sum(col, acc, j_lo, j_hi):
        for j in range(j_lo, j_hi):
            acc = acc + _shifted_rows(pbuf, pshift, POOL_HALO - j, tm, slice(col, col + LANES))
        return acc

    pooled = []
    for tile, (w_lo, w_hi) in enumerate(((POOL_WINDOWS[0], POOL_WINDOWS[1]),
                                         (POOL_WINDOWS[2], POOL_WINDOWS[3]))):
        col = tile * LANES
        lanes = slice(col, col + LANES)
        for sh in sorted({(POOL_HALO - j) % SUBLANES for j in range(1, w_hi)} - {0}):
            pshift[sh - 1, :, lanes] = pbuf[sh:sh + POOL_HALO + tm - SUBLANES, lanes]
        self_v = pv[:, col:col + LANES]
        s_lo = window_sum(col, self_v, 1, w_lo)
        s_hi = window_sum(col, s_lo, w_lo, w_hi)
        width = jnp.where(low_half, w_lo, w_hi)
        cnt = jnp.minimum(pos + 1, width).astype(F32)
        pooled.append(jnp.where(low_half, s_lo, s_hi) / cnt - self_v)
    p = jnp.concatenate(pooled, axis=1).astype(BF16)
    ya = _dot(p, pbd_ref[...]) * psc_ref[...] * jax.nn.silu(zp[:, POOL_W:])
    u_ref[0, :, 0:POOL_W] = ya.astype(BF16)
    pbuf[0:POOL_HALO, :] = pbuf[pl.ds(valid, POOL_HALO), :]

    qc = qc_ref[...]
    qs = qs_ref[...]
    half = QK_ROPE // 2
    for hd in range(MLA_HEADS):
        src = hd * QK_DIM
        dst = hd * HEAD_PAD
        t1 = qa[src + QK_NOPE:src + QK_NOPE + half, :]
        t2 = qa[src + QK_NOPE + half:src + QK_DIM, :]
        q_ref[0, dst:dst + QK_NOPE, :] = (qa[src:src + QK_NOPE, :] * Q_SCALE).astype(BF16)
        q_ref[0, dst + QK_NOPE:dst + QK_NOPE + half, :] = (t1 * qc - t2 * qs).astype(BF16)
        q_ref[0, dst + QK_NOPE + half:dst + QK_DIM, :] = (t1 * qs + t2 * qc).astype(BF16)
        q_ref[0, dst + QK_DIM:dst + HEAD_PAD, :] = jnp.zeros((HEAD_PAD - QK_DIM, tm), BF16)
    zk = zc[:, C_KR - C_CQ:C_MG - C_CQ]
    k_rope = zk * ck_ref[...] + pltpu.roll(zk, LANES - QK_ROPE, 1) * sk_ref[...]
    for hd in range(MLA_HEADS):
        sl = slice(hd * HEAD_PAD, (hd + 1) * HEAD_PAD)
        k_ref[0, 0, :, sl] = (kn[:, sl] + k_rope).astype(BF16)
    v_ref[0, 0] = vt.astype(BF16)
    smg_ref[0] = jax.nn.silu(zm).astype(BF16)

    sbuf[SC_HALO:SC_HALO + tm, :] = zs[:, SC_W:2 * SC_W] * zs[:, 2 * SC_W:3 * SC_W]
    conv = jnp.zeros((tm, SC_W), F32)
    base = SC_HALO - (SC_K - 1)
    for kk in range(SC_K):
        conv = conv + sw_ref[kk:kk + 1, :] * sbuf[base + kk:base + kk + tm, :]
    yd = zs[:, 0:SC_W] * conv * jax.nn.silu(zs[:, 3 * SC_W:4 * SC_W])
    u_ref[0, :, POOL_W + CONF_W:U_W] = yd.astype(BF16)
    sbuf[0:SC_HALO, :] = sbuf[pl.ds(valid, SC_HALO), :]


def _layer_spec(arr, layer):
    return pl.BlockSpec((None,) + arr.shape[1:], lambda *_: (layer, 0, 0))


def _pre_call(x_main, x_meta, meta_tile_idx, layer, wts, tabs, bsz, n_seq_tiles, copy_x):
    nt = n_seq_tiles + 1
    lp = nt * ROW_TILE
    meta_b = (lambda b: b) if x_meta.shape[0] == bsz else (lambda b: 0)
    seq_tile = lambda t: jnp.maximum(t - 1, 0)
    stored = lambda t: (t + n_seq_tiles) % nt
    consts = [wts[n] for n in ("png", "w1", "pbd", "psc", "qng", "wq", "kvg", "wk", "wv")]
    consts2 = [wts[n] for n in ("cw", "cb", "lg", "lb", "sw")]
    in_specs = ([pl.BlockSpec((1, ROW_TILE, D_MODEL), lambda b, t: (b, seq_tile(t), 0)),
                 pl.BlockSpec((1, ROW_TILE, D_MODEL), lambda b, t: (meta_b(b), meta_tile_idx, 0))]
                + [_layer_spec(c, layer) for c in consts]
                + [pl.BlockSpec((QK_ROPE // 2, ROW_TILE), lambda b, t: (0, stored(t)))] * 2
                + [pl.BlockSpec((ROW_TILE, LANES), lambda b, t: (stored(t), 0))] * 2
                + [_layer_spec(c, layer) for c in consts2])
    row_spec = lambda w: pl.BlockSpec((1, ROW_TILE, w), lambda b, t: (b, stored(t), 0))
    out_specs = [pl.BlockSpec((1, QK_W, ROW_TILE), lambda b, t: (b, 0, stored(t))),
                 pl.BlockSpec((1, 1, ROW_TILE, QK_W), lambda b, t: (b, stored(t), 0, 0)),
                 pl.BlockSpec((1, 1, MLA_W, ROW_TILE), lambda b, t: (b, stored(t), 0, 0)),
                 row_spec(MLA_W), row_spec(U_W)]
    out_shape = [jax.ShapeDtypeStruct((bsz, QK_W, lp), BF16),
                 jax.ShapeDtypeStruct((bsz, nt, ROW_TILE, QK_W), BF16),
                 jax.ShapeDtypeStruct((bsz, nt, MLA_W, ROW_TILE), BF16),
                 jax.ShapeDtypeStruct((bsz, lp, MLA_W), BF16),
                 jax.ShapeDtypeStruct((bsz, lp, U_W), BF16)]
    if copy_x:
        out_specs.append(row_spec(D_MODEL))
        out_shape.append(jax.ShapeDtypeStruct((bsz, lp, D_MODEL), F32))
    return pl.pallas_call(
        functools.partial(_pre_kernel, copy_x),
        grid=(bsz, nt),
        in_specs=in_specs,
        out_specs=out_specs,
        out_shape=out_shape,
        scratch_shapes=[pltpu.VMEM((POOL_HALO + ROW_TILE, POOL_W), F32),
                        pltpu.VMEM((CONF_HALO + ROW_TILE, CONF_W), F32),
                        pltpu.VMEM((SC_HALO + ROW_TILE, SC_W), F32),
                        pltpu.VMEM((SUBLANES - 1, POOL_HALO + ROW_TILE - SUBLANES, POOL_W), F32),
                        pltpu.VMEM((SUBLANES - 1, CONF_HALO + ROW_TILE - SUBLANES, CONF_W), F32)],
        compiler_params=pltpu.CompilerParams(dimension_semantics=("arbitrary", "arbitrary"),
                                             vmem_limit_bytes=VMEM_LIMIT),
        name="mixer_pre",
    )(x_main, x_meta, *consts, *tabs, *consts2)


def _attn_kernel(qt_ref, k_ref, vt_ref, o_ref, acc_sc, s_even, s_odd):
    tq = ROW_TILE
    i = pl.program_id(1)
    n_seq = k_ref.shape[1] - 1
    is_seq = i < n_seq
    key_idx = lax.broadcasted_iota(jnp.int32, (tq, tq), 0)
    qry_idx = lax.broadcasted_iota(jnp.int32, (tq, tq), 1)
    causal = key_idx <= qry_idx
    ones_rows = jnp.ones((SUM_ROWS, tq), BF16)

    def scores_into(s_ref, j):
        for hd in range(MLA_HEADS):
            hsl = slice(hd * HEAD_PAD, (hd + 1) * HEAD_PAD)
            s_ref[hd] = _dot(k_ref[0, j, :, hsl], qt_ref[0, hsl, :])

    mkey = lax.broadcasted_iota(jnp.int32, (N_META, tq), 0)
    mqry = lax.broadcasted_iota(jnp.int32, (N_META, tq), 1)
    meta_ok = is_seq | (mkey <= mqry)
    meta_scores = []
    for hd in range(MLA_HEADS):
        hsl = slice(hd * HEAD_PAD, (hd + 1) * HEAD_PAD)
        meta_scores.append(_dot(k_ref[0, n_seq, 0:N_META, hsl], qt_ref[0, hsl, :]))
    scores_into(s_even, 0)
    zero_rows = jnp.zeros((META_KEYS - N_META, tq), BF16)
    maxes = []
    for hd in range(MLA_HEADS):
        vsl = slice(hd * V_DIM, (hd + 1) * V_DIM)
        s = jnp.where(meta_ok, meta_scores[hd], -jnp.inf)
        m0 = jnp.max(s, axis=0, keepdims=True)
        p = jnp.concatenate([jnp.exp2(s - m0).astype(BF16), zero_rows], axis=0)
        v_ext = jnp.concatenate([vt_ref[0, n_seq, vsl, 0:META_KEYS], ones_rows[:, 0:META_KEYS]], axis=0)
        acc_sc[hd] = _dot(v_ext, p)
        maxes.append(m0)
    maxes = tuple(maxes)

    def consume(s_ref, j, maxes, masked):
        new_maxes = []
        for hd in range(MLA_HEADS):
            vsl = slice(hd * V_DIM, (hd + 1) * V_DIM)
            s = s_ref[hd]
            if masked:
                s = jnp.where(causal, s, -jnp.inf)
            m_new = jnp.maximum(maxes[hd], jnp.max(s, axis=0, keepdims=True))
            alpha = jnp.exp2(maxes[hd] - m_new)
            p = jnp.exp2(s - m_new).astype(BF16)
            v_ext = jnp.concatenate([vt_ref[0, j, vsl, :], ones_rows], axis=0)
            acc_sc[hd] = alpha * acc_sc[hd] + _dot(v_ext, p)
            new_maxes.append(m_new)
        return tuple(new_maxes)

    def finish():
        for pair in range(MLA_HEADS // 2):
            halves = []
            for hd in (2 * pair, 2 * pair + 1):
                acc = acc_sc[hd]
                halves.append(acc[0:V_DIM, :] / acc[V_DIM:V_DIM + 1, :])
            o_ref[0, :, pair * LANES:(pair + 1) * LANES] = jnp.concatenate(halves, axis=0).T.astype(BF16)

    def two_blocks(jj, maxes):
        j = 2 * jj
        scores_into(s_odd, j + 1)
        maxes = consume(s_even, j, maxes, False)
        scores_into(s_even, j + 2)
        return consume(s_odd, j + 1, maxes, False)

    maxes = lax.fori_loop(0, jnp.where(is_seq, i // 2, 0), two_blocks, maxes)

    @pl.when(is_seq & (i % 2 == 0))
    def _():
        consume(s_even, i, maxes, True)
        finish()

    @pl.when(is_seq & (i % 2 == 1))
    def _():
        scores_into(s_odd, i)
        consume(s_odd, i, consume(s_even, i - 1, maxes, False), True)
        finish()

    @pl.when(jnp.logical_not(is_seq))
    def _():
        finish()


def _attn_call(qt, k, vt):
    bsz, nt = k.shape[0], k.shape[1]
    lp = nt * ROW_TILE
    return pl.pallas_call(
        _attn_kernel,
        grid=(bsz, nt),
        in_specs=[pl.BlockSpec((1, QK_W, ROW_TILE), lambda b, i: (b, 0, i)),
                  pl.BlockSpec((1, nt, ROW_TILE, QK_W), lambda b, i: (b, 0, 0, 0)),
                  pl.BlockSpec((1, nt, MLA_W, ROW_TILE), lambda b, i: (b, 0, 0, 0))],
        out_specs=pl.BlockSpec((1, ROW_TILE, MLA_W), lambda b, i: (b, i, 0)),
        out_shape=jax.ShapeDtypeStruct((bsz, lp, MLA_W), BF16),
        scratch_shapes=[pltpu.VMEM((MLA_HEADS, V_DIM + SUM_ROWS, ROW_TILE), F32),
                        pltpu.VMEM((MLA_HEADS, ROW_TILE, ROW_TILE), F32),
                        pltpu.VMEM((MLA_HEADS, ROW_TILE, ROW_TILE), F32)],
        compiler_params=pltpu.CompilerParams(dimension_semantics=("arbitrary", "arbitrary"),
                                             vmem_limit_bytes=VMEM_LIMIT),
        name="mla_attention",
    )(qt, k, vt)


def _post_kernel(x_ref, u_ref, o_ref, smg_ref, png_ref, wg_ref, gb_ref, wpool_ref, wmla_ref, wconf_ref,
                 wsc_ref, wo_ref, pog_ref, out_ref):
    x = x_ref[0]
    h = _rms(x, png_ref[...]).astype(BF16)
    u = u_ref[0]
    ub = (o_ref[0].astype(F32) * smg_ref[0].astype(F32)).astype(BF16)
    branch_in = (u[:, 0:POOL_W], ub, u[:, POOL_W:POOL_W + CONF_W], u[:, POOL_W + CONF_W:])
    branch_w = (wpool_ref, wmla_ref, wconf_ref, wsc_ref)
    m = None
    for br in range(N_BRANCH):
        csl = slice(br * D_MODEL, (br + 1) * D_MODEL)
        gate = jax.nn.sigmoid(_dot(h, wg_ref[:, csl]) + gb_ref[:, csl])
        y = gate * _dot(branch_in[br], branch_w[br][...])
        m = y if m is None else m + y
    mo = _dot(m.astype(BF16), wo_ref[...])
    out_ref[0] = x + _rms(mo, pog_ref[...])


def _post_call(x, u, o, smg, layer, wts, rows_out):
    groups = x.shape[0]
    row_spec = lambda w: pl.BlockSpec((1, POST_TILE, w), lambda g, r: (g, r, 0))
    consts = [wts[n] for n in ("png", "wg", "gb", "wpool", "wmla", "wconf", "wsc", "wo", "pog")]
    return pl.pallas_call(
        _post_kernel,
        grid=(groups, rows_out // POST_TILE),
        in_specs=[row_spec(D_MODEL), row_spec(U_W), row_spec(MLA_W), row_spec(MLA_W)]
        + [_layer_spec(c, layer) for c in consts],
        out_specs=row_spec(D_MODEL),
        out_shape=jax.ShapeDtypeStruct((groups, rows_out, D_MODEL), F32),
        compiler_params=pltpu.CompilerParams(dimension_semantics=("arbitrary", "arbitrary"),
                                             vmem_limit_bytes=VMEM_LIMIT),
        name="mixer_post",
    )(x, u, o, smg, *consts)


def _rope_tables(seq):
    inv = 1.0 / (ROPE_THETA ** (jnp.arange(0, QK_ROPE, 2, dtype=F32) / QK_ROPE))
    pos = jnp.concatenate([N_META + jnp.arange(seq, dtype=F32), jnp.arange(ROW_TILE, dtype=F32)])
    ang = pos[:, None] * inv[None, :]
    cos, sin = jnp.cos(ang), jnp.sin(ang)
    n = pos.shape[0]
    zeros = lambda w: jnp.zeros((n, w), F32)
    tail = HEAD_PAD - QK_DIM
    ck = jnp.concatenate([zeros(QK_NOPE), cos, cos, zeros(tail)], axis=1)
    sk = jnp.concatenate([zeros(QK_NOPE), -sin, sin, zeros(tail)], axis=1)
    return (cos * Q_SCALE).T, (sin * Q_SCALE).T, ck, sk


def _swap_halves(w):
    half = QK_ROPE // 2
    return jnp.concatenate([w[..., half:], w[..., :half]], axis=-1)


def _prepare_weights(p):
    w_in = p["w_in"]
    cuts = [0] + [int(s) for s in np.cumsum(IN_SPLITS)]
    (w_pv, w_pg, w_cq, w_ckv, w_kr, w_mg, w_cu, w_cg, w_sb, w_sg, w_gl) = [
        w_in[:, :, a:b] for a, b in zip(cuts[:-1], cuts[1:])]
    kr_tile = jnp.concatenate([jnp.zeros((DEPTH, D_MODEL, QK_NOPE), F32), w_kr, _swap_halves(w_kr)], axis=-1)
    w1 = jnp.concatenate([w_pv, w_pg, w_cq, w_ckv, kr_tile, w_mg, w_cu, w_cg, w_sb, w_sg], axis=-1)

    pbd = jnp.zeros((DEPTH, POOL_W, POOL_W), F32)
    for g in range(len(POOL_WINDOWS)):
        sl = slice(g * POOL_GROUP, (g + 1) * POOL_GROUP)
        pbd = pbd.at[:, sl, sl].set(p["pool_w"][:, g])

    wq_t = jnp.swapaxes(p["w_uq"], 1, 2)
    wukv = p["w_ukv"].reshape(DEPTH, KV_RANK, MLA_HEADS, QK_NOPE + V_DIM)
    wk = jnp.concatenate([wukv[..., :QK_NOPE], jnp.zeros(wukv.shape[:3] + (HEAD_PAD - QK_NOPE,), F32)],
                         axis=-1).reshape(DEPTH, KV_RANK, QK_W)
    wv_t = jnp.swapaxes(wukv[..., QK_NOPE:].reshape(DEPTH, KV_RANK, MLA_W), 1, 2)
    row = lambda a: a[:, None, :]
    bf = lambda a: a.astype(BF16)
    return dict(
        png=row(p["pre_norm_g"]), w1=bf(w1), pbd=bf(pbd), psc=row(p["pool_scale"]), qng=row(p["q_norm_g"]),
        wq=bf(wq_t), kvg=row(p["kv_norm_g"]), wk=bf(wk), wv=bf(wv_t), cw=p["conf_dw_w"],
        cb=row(p["conf_dw_b"]), lg=row(p["conf_ln_g"]), lb=row(p["conf_ln_b"]), sw=p["sc_dw_w"],
        wg=bf(w_gl), gb=row(p["gate_bias"]), wpool=bf(p["w_out_pool"]), wmla=bf(p["w_out_mla"]),
        wconf=bf(p["w_out_conf"]), wsc=bf(p["w_out_sc"]), wo=bf(p["w_o"]), pog=row(p["post_norm_g"]))


def kernel(x, meta_tokens, pre_norm_g, w_in, gate_bias, pool_w, pool_scale, w_out_pool, q_norm_g, w_uq,
           kv_norm_g, w_ukv, w_out_mla, conf_dw_w, conf_dw_b, conf_ln_g, conf_ln_b, w_out_conf, sc_dw_w,
           w_out_sc, w_o, post_norm_g):
    bsz, seq, _ = x.shape
    assert seq % POST_TILE == 0 and meta_tokens.shape[0] == N_META
    n_seq_tiles = seq // ROW_TILE
    lp = seq + ROW_TILE
    assert (bsz * lp) % POST_TILE == 0
    wts = _prepare_weights(dict(
        w_in=w_in, pre_norm_g=pre_norm_g, gate_bias=gate_bias, pool_w=pool_w, pool_scale=pool_scale,
        w_out_pool=w_out_pool, q_norm_g=q_norm_g, w_uq=w_uq, kv_norm_g=kv_norm_g, w_ukv=w_ukv,
        w_out_mla=w_out_mla, conf_dw_w=conf_dw_w, conf_dw_b=conf_dw_b, conf_ln_g=conf_ln_g,
        conf_ln_b=conf_ln_b, w_out_conf=w_out_conf, sc_dw_w=sc_dw_w, w_out_sc=w_out_sc, w_o=w_o,
        post_norm_g=post_norm_g))
    tabs = _rope_tables(seq)
    meta_tile = jnp.concatenate([meta_tokens.astype(x.dtype),
                                 jnp.zeros((ROW_TILE - N_META, D_MODEL), x.dtype)])[None]

    h_res = None
    for layer in range(DEPTH):
        if layer == 0:
            qt, k, vt, smg, u, h_res = _pre_call(x, meta_tile, 0, layer, wts, tabs, bsz, n_seq_tiles, True)
        else:
            qt, k, vt, smg, u = _pre_call(h_res, h_res, n_seq_tiles, layer, wts, tabs, bsz, n_seq_tiles,
                                          False)
        o = _attn_call(qt, k, vt)
        if layer < DEPTH - 1:
            flat = lambda a: a.reshape(1, bsz * lp, a.shape[-1])
            h_res = _post_call(flat(h_res), flat(u), flat(o), flat(smg), layer, wts,
                               bsz * lp).reshape(bsz, lp, D_MODEL)
        else:
            h_res = _post_call(h_res, u, o, smg, layer, wts, seq)
    return h_res
```

```python
import functools

import jax
import jax.numpy as jnp
import numpy as np
from jax import lax
from jax.experimental import pallas as pl
from jax.experimental.pallas import tpu as pltpu

F32 = jnp.float32
BF16 = jnp.bfloat16

D_MODEL = 1024
DEPTH = 4
N_META = 16
EPS = 1e-6
N_BRANCH = 4
POOL_WINDOWS = (2, 4, 8, 16)
POOL_GROUP = 64
POOL_W = 256
MLA_HEADS = 8
QK_NOPE = 64
QK_ROPE = 32
QK_DIM = QK_NOPE + QK_ROPE
V_DIM = 64
Q_RANK = 256
KV_RANK = 128
ROPE_THETA = 10000.0
MLA_W = MLA_HEADS * V_DIM
CONF_W = 256
CONF_K = 31
SC_W = 256
SC_K = 3
IN_SPLITS = (POOL_W, POOL_W, Q_RANK, KV_RANK, QK_ROPE, MLA_W, 2 * CONF_W, CONF_W, 3 * SC_W, SC_W,
             N_BRANCH * D_MODEL)
U_W = POOL_W + CONF_W + SC_W

LANES = 128
SUBLANES = 8
HEAD_PAD = LANES
QK_W = MLA_HEADS * HEAD_PAD
ROW_TILE = 256
POST_TILE = 2 * ROW_TILE
POOL_HALO = 16
CONF_HALO = 32
SC_HALO = 8
SUM_ROWS = 16
QK_LEAD = 3
META_KEYS = LANES
VMEM_LIMIT = 56 * 1024 * 1024
LOG2_E = 1.4426950408889634
Q_SCALE = QK_DIM ** -0.5 * LOG2_E

C_PV, C_PG, C_CQ, C_CKV, C_KR, C_MG, C_CU, C_CG, C_SB, C_SG, C_END = (
    0, 256, 512, 768, 896, 1024, 1536, 2048, 2304, 3072, 3328)


def _rms(x, g):
    return x * lax.rsqrt(jnp.mean(x * x, axis=-1, keepdims=True) + EPS) * g


def _dot(a, b):
    return jnp.dot(a, b, preferred_element_type=F32)


def _dot_nt(a, b):
    return lax.dot_general(a, b, (((1,), (1,)), ((), ())), preferred_element_type=F32)


def _shifted_rows(buf, shifted, off, rows, lanes):
    tile, sh = divmod(off, SUBLANES)
    if sh == 0:
        return buf[off:off + rows, lanes]
    return shifted[sh - 1, tile * SUBLANES:tile * SUBLANES + rows, lanes]


def _pre_kernel(copy_x, x_ref, xm_ref, png_ref, w1_ref, pbd_ref, psc_ref, qng_ref, wq_ref, kvg_ref, wk_ref,
                wv_ref, qc_ref, qs_ref, ck_ref, sk_ref, cw_ref, cb_ref, lg_ref, lb_ref, sw_ref, *rest):
    if copy_x:
        q_ref, k_ref, v_ref, smg_ref, u_ref, xo_ref, pbuf, cbuf, sbuf, pshift, cshift = rest
    else:
        q_ref, k_ref, v_ref, smg_ref, u_ref, pbuf, cbuf, sbuf, pshift, cshift = rest
    tm = ROW_TILE
    t = pl.program_id(1)
    is_meta = t == 0

    @pl.when(is_meta)
    def _():
        pbuf[0:POOL_HALO, :] = jnp.zeros((POOL_HALO, POOL_W), F32)
        cbuf[0:CONF_HALO, :] = jnp.zeros((CONF_HALO, CONF_W), F32)
        sbuf[0:SC_HALO, :] = jnp.zeros((SC_HALO, SC_W), F32)

    x = jnp.where(is_meta, xm_ref[0], x_ref[0])
    if copy_x:
        xo_ref[0] = x
    h = _rms(x, png_ref[...]).astype(BF16)
    valid = pl.multiple_of(jnp.where(is_meta, N_META, tm), SUBLANES)

    def proj(lo, hi):
        return _dot(h, w1_ref[:, lo:hi])

    zu = proj(C_CU, C_SB)
    glu = zu[:, 0:CONF_W] * jax.nn.sigmoid(zu[:, CONF_W:2 * CONF_W])
    cbuf[CONF_HALO:CONF_HALO + tm, :] = glu
    for sh in range(1, SUBLANES):
        cshift[sh - 1] = cbuf[sh:sh + CONF_HALO + tm - SUBLANES, :]

    def conv_taps(acc, k_lo, k_hi):
        base = CONF_HALO - (CONF_K - 1)
        for kk in range(k_lo, k_hi):
            acc = acc + cw_ref[kk:kk + 1, :] * _shifted_rows(cbuf, cshift, base + kk, tm, slice(None))
        return acc

    tap_cuts = (0, 8, 16, 24, CONF_K)
    acc = conv_taps(jnp.zeros((tm, CONF_W), F32) + cb_ref[...], tap_cuts[0], tap_cuts[1])
    zp = proj(C_PV, C_CQ)
    acc = conv_taps(acc, tap_cuts[1], tap_cuts[2])
    zc = proj(C_CQ, C_MG)
    cqn = _rms(zc[:, 0:Q_RANK], qng_ref[...]).astype(BF16)
    ckvn = _rms(zc[:, Q_RANK:Q_RANK + KV_RANK], kvg_ref[...]).astype(BF16)
    qa = _dot_nt(wq_ref[...], cqn)
    kn = _dot(ckvn, wk_ref[...])
    vt = _dot_nt(wv_ref[...], ckvn)
    acc = conv_taps(acc, tap_cuts[2], tap_cuts[3])
    zm = proj(C_MG, C_CU)
    zs = proj(C_SB, C_END)
    acc = conv_taps(acc, tap_cuts[3], tap_cuts[4])
    mu = jnp.mean(acc, axis=-1, keepdims=True)
    cen = acc - mu
    var = jnp.mean(cen * cen, axis=-1, keepdims=True)
    yc = cen * lax.rsqrt(var + EPS) * lg_ref[...] + lb_ref[...]
    yc = jax.nn.silu(yc) * jax.nn.silu(zu[:, 2 * CONF_W:3 * CONF_W])
    u_ref[0, :, POOL_W:POOL_W + CONF_W] = yc.astype(BF16)
    cbuf[0:CONF_HALO, :] = cbuf[pl.ds(valid, CONF_HALO), :]

    pv = zp[:, :POOL_W]
    pbuf[POOL_HALO:POOL_HALO + tm, :] = pv
    first_pos = jnp.where(is_meta, 0, N_META + (t - 1) * tm)
    pos = lax.broadcasted_iota(jnp.int32, (tm, LANES), 0) + first_pos
    low_half = lax.broadcasted_iota(jnp.int32, (tm, LANES), 1) < POOL_GROUP

    def window_sum(col, acc, j_lo, j_hi):
        for j in range(j_lo, j_hi):
            acc = acc + _shifted_rows(pbuf, pshift, POOL_HALO - j, tm, slice(col, col + LANES))
        return acc

    pooled = []
    for tile, (w_lo, w_hi) in enumerate(((POOL_WINDOWS[0], POOL_WINDOWS[1]),
                                         (POOL_WINDOWS[2], POOL_WINDOWS[3]))):
        col = tile * LANES
        lanes = slice(col, col + LANES)
        for sh in sorted({(POOL_HALO - j) % SUBLANES for j in range(1, w_hi)} - {0}):
            pshift[sh - 1, :, lanes] = pbuf[sh:sh + POOL_HALO + tm - SUBLANES, lanes]
        self_v = pv[:, col:col + LANES]
        s_lo = window_sum(col, self_v, 1, w_lo)
        s_hi = window_sum(col, s_lo, w_lo, w_hi)
        width = jnp.where(low_half, w_lo, w_hi)
        cnt = jnp.minimum(pos + 1, width).astype(F32)
        pooled.append(jnp.where(low_half, s_lo, s_hi) / cnt - self_v)
    p = jnp.concatenate(pooled, axis=1).astype(BF16)
    ya = _dot(p, pbd_ref[...]) * psc_ref[...] * jax.nn.silu(zp[:, POOL_W:])
    u_ref[0, :, 0:POOL_W] = ya.astype(BF16)
    pbuf[0:POOL_HALO, :] = pbuf[pl.ds(valid, POOL_HALO), :]

    qc = qc_ref[...]
    qs = qs_ref[...]
    half = QK_ROPE // 2
    for hd in range(MLA_HEADS):
        src = hd * QK_DIM
        dst = hd * HEAD_PAD
        t1 = qa[src + QK_NOPE:src + QK_NOPE + half, :]
        t2 = qa[src + QK_NOPE + half:src + QK_DIM, :]
        q_ref[0, dst:dst + QK_NOPE, :] = (qa[src:src + QK_NOPE, :] * Q_SCALE).astype(BF16)
        q_ref[0, dst + QK_NOPE:dst + QK_NOPE + half, :] = (t1 * qc - t2 * qs).astype(BF16)
        q_ref[0, dst + QK_NOPE + half:dst + QK_DIM, :] = (t1 * qs + t2 * qc).astype(BF16)
        q_ref[0, dst + QK_DIM:dst + HEAD_PAD, :] = jnp.zeros((HEAD_PAD - QK_DIM, tm), BF16)
    zk = zc[:, C_KR - C_CQ:C_MG - C_CQ]
    k_rope = zk * ck_ref[...] + pltpu.roll(zk, LANES - QK_ROPE, 1) * sk_ref[...]
    for hd in range(MLA_HEADS):
        sl = slice(hd * HEAD_PAD, (hd + 1) * HEAD_PAD)
        k_ref[0, 0, :, sl] = (kn[:, sl] + k_rope).astype(BF16)
    v_ref[0, 0] = vt.astype(BF16)
    smg_ref[0] = jax.nn.silu(zm).astype(BF16)

    sbuf[SC_HALO:SC_HALO + tm, :] = zs[:, SC_W:2 * SC_W] * zs[:, 2 * SC_W:3 * SC_W]
    conv = jnp.zeros((tm, SC_W), F32)
    base = SC_HALO - (SC_K - 1)
    for kk in range(SC_K):
        conv = conv + sw_ref[kk:kk + 1, :] * sbuf[base + kk:base + kk + tm, :]
    yd = zs[:, 0:SC_W] * conv * jax.nn.silu(zs[:, 3 * SC_W:4 * SC_W])
    u_ref[0, :, POOL_W + CONF_W:U_W] = yd.astype(BF16)
    sbuf[0:SC_HALO, :] = sbuf[pl.ds(valid, SC_HALO), :]


def _layer_spec(arr, layer):
    return pl.BlockSpec((None,) + arr.shape[1:], lambda *_: (layer, 0, 0))


def _pre_call(x_main, x_meta, meta_tile_idx, layer, wts, tabs, bsz, n_seq_tiles, copy_x):
    nt = n_seq_tiles + 1
    lp = nt * ROW_TILE
    meta_b = (lambda b: b) if x_meta.shape[0] == bsz else (lambda b: 0)
    seq_tile = lambda t: jnp.maximum(t - 1, 0)
    stored = lambda t: (t + n_seq_tiles) % nt
    consts = [wts[n] for n in ("png", "w1", "pbd", "psc", "qng", "wq", "kvg", "wk", "wv")]
    consts2 = [wts[n] for n in ("cw", "cb", "lg", "lb", "sw")]
    in_specs = ([pl.BlockSpec((1, ROW_TILE, D_MODEL), lambda b, t: (b, seq_tile(t), 0)),
                 pl.BlockSpec((1, ROW_TILE, D_MODEL), lambda b, t: (meta_b(b), meta_tile_idx, 0))]
                + [_layer_spec(c, layer) for c in consts]
                + [pl.BlockSpec((QK_ROPE // 2, ROW_TILE), lambda b, t: (0, stored(t)))] * 2
                + [pl.BlockSpec((ROW_TILE, LANES), lambda b, t: (stored(t), 0))] * 2
                + [_layer_spec(c, layer) for c in consts2])
    row_spec = lambda w: pl.BlockSpec((1, ROW_TILE, w), lambda b, t: (b, stored(t), 0))
    out_specs = [pl.BlockSpec((1, QK_W, ROW_TILE), lambda b, t: (b, 0, stored(t))),
                 pl.BlockSpec((1, 1, ROW_TILE, QK_W), lambda b, t: (b, stored(t), 0, 0)),
                 pl.BlockSpec((1, 1, MLA_W, ROW_TILE), lambda b, t: (b, stored(t), 0, 0)),
                 row_spec(MLA_W), row_spec(U_W)]
    out_shape = [jax.ShapeDtypeStruct((bsz, QK_W, lp), BF16),
                 jax.ShapeDtypeStruct((bsz, nt, ROW_TILE, QK_W), BF16),
                 jax.ShapeDtypeStruct((bsz, nt, MLA_W, ROW_TILE), BF16),
                 jax.ShapeDtypeStruct((bsz, lp, MLA_W), BF16),
                 jax.ShapeDtypeStruct((bsz, lp, U_W), BF16)]
    if copy_x:
        out_specs.append(row_spec(D_MODEL))
        out_shape.append(jax.ShapeDtypeStruct((bsz, lp, D_MODEL), F32))
    return pl.pallas_call(
        functools.partial(_pre_kernel, copy_x),
        grid=(bsz, nt),
        in_specs=in_specs,
        out_specs=out_specs,
        out_shape=out_shape,
        scratch_shapes=[pltpu.VMEM((POOL_HALO + ROW_TILE, POOL_W), F32),
                        pltpu.VMEM((CONF_HALO + ROW_TILE, CONF_W), F32),
                        pltpu.VMEM((SC_HALO + ROW_TILE, SC_W), F32),
                        pltpu.VMEM((SUBLANES - 1, POOL_HALO + ROW_TILE - SUBLANES, POOL_W), F32),
                        pltpu.VMEM((SUBLANES - 1, CONF_HALO + ROW_TILE - SUBLANES, CONF_W), F32)],
        compiler_params=pltpu.CompilerParams(dimension_semantics=("arbitrary", "arbitrary"),
                                             vmem_limit_bytes=VMEM_LIMIT),
        name="mixer_pre",
    )(x_main, x_meta, *consts, *tabs, *consts2)


def _attn_kernel(qt_ref, k_ref, vt_ref, o_ref, acc_sc, s_even, s_odd):
    tq = ROW_TILE
    n_seq = k_ref.shape[1] - 1
    is_seq = pl.program_id(1) > 0
    i = pl.program_id(1) - 1
    key_idx = lax.broadcasted_iota(jnp.int32, (tq, tq), 0)
    qry_idx = lax.broadcasted_iota(jnp.int32, (tq, tq), 1)
    causal = key_idx <= qry_idx
    ones_rows = jnp.ones((SUM_ROWS, tq), BF16)

    def score_head(s_ref, j, hd):
        hsl = slice(hd * HEAD_PAD, (hd + 1) * HEAD_PAD)
        s_ref[hd] = _dot(k_ref[0, j, :, hsl], qt_ref[0, hsl, :])

    def scores_into(s_ref, j):
        for hd in range(MLA_HEADS):
            score_head(s_ref, j, hd)

    mkey = lax.broadcasted_iota(jnp.int32, (N_META, tq), 0)
    mqry = lax.broadcasted_iota(jnp.int32, (N_META, tq), 1)
    meta_ok = is_seq | (mkey <= mqry)
    meta_scores = []
    for hd in range(MLA_HEADS):
        hsl = slice(hd * HEAD_PAD, (hd + 1) * HEAD_PAD)
        meta_scores.append(_dot(k_ref[0, n_seq, 0:N_META, hsl], qt_ref[0, hsl, :]))
    scores_into(s_even, 0)
    zero_rows = jnp.zeros((META_KEYS - N_META, tq), BF16)
    maxes = []
    for hd in range(MLA_HEADS):
        vsl = slice(hd * V_DIM, (hd + 1) * V_DIM)
        s = jnp.where(meta_ok, meta_scores[hd], -jnp.inf)
        m0 = jnp.max(s, axis=0, keepdims=True)
        p = jnp.concatenate([jnp.exp2(s - m0).astype(BF16), zero_rows], axis=0)
        v_ext = jnp.concatenate([vt_ref[0, n_seq, vsl, 0:META_KEYS], ones_rows[:, 0:META_KEYS]], axis=0)
        acc_sc[hd] = _dot(v_ext, p)
        maxes.append(m0)
    maxes = tuple(maxes)

    def consume_head(s_ref, j, hd, m_old, masked):
        vsl = slice(hd * V_DIM, (hd + 1) * V_DIM)
        s = s_ref[hd]
        if masked:
            s = jnp.where(causal, s, -jnp.inf)
        m_new = jnp.maximum(m_old, jnp.max(s, axis=0, keepdims=True))
        alpha = jnp.exp2(m_old - m_new)
        p = jnp.exp2(s - m_new).astype(BF16)
        v_ext = jnp.concatenate([vt_ref[0, j, vsl, :], ones_rows], axis=0)
        acc_sc[hd] = alpha * acc_sc[hd] + _dot(v_ext, p)
        return m_new

    def consume(s_ref, j, maxes, masked, s_next=None, j_next=None):
        new_maxes = []
        if s_next is not None:
            for hd in range(QK_LEAD):
                score_head(s_next, j_next, hd)
        for hd in range(MLA_HEADS):
            new_maxes.append(consume_head(s_ref, j, hd, maxes[hd], masked))
            if s_next is not None and hd + QK_LEAD < MLA_HEADS:
                score_head(s_next, j_next, hd + QK_LEAD)
        return tuple(new_maxes)

    def finish():
        for pair in range(MLA_HEADS // 2):
            halves = []
            for hd in (2 * pair, 2 * pair + 1):
                acc = acc_sc[hd]
                halves.append(acc[0:V_DIM, :] / acc[V_DIM:V_DIM + 1, :])
            o_ref[0, :, pair * LANES:(pair + 1) * LANES] = jnp.concatenate(halves, axis=0).T.astype(BF16)

    def two_blocks(jj, maxes):
        j = 2 * jj
        maxes = consume(s_even, j, maxes, False, s_odd, j + 1)
        return consume(s_odd, j + 1, maxes, False, s_even, j + 2)

    def four_blocks(jj, maxes):
        return two_blocks(2 * jj + 1, two_blocks(2 * jj, maxes))

    n_full = jnp.where(is_seq, i, 0)
    maxes = lax.fori_loop(0, n_full // 4, four_blocks, maxes)
    maxes = lax.fori_loop(2 * (n_full // 4), n_full // 2, two_blocks, maxes)

    @pl.when(is_seq & (i % 2 == 0))
    def _():
        consume(s_even, i, maxes, True)
        finish()

    @pl.when(is_seq & (i % 2 == 1))
    def _():
        consume(s_odd, i, consume(s_even, i - 1, maxes, False, s_odd, i), True)
        finish()

    @pl.when(jnp.logical_not(is_seq))
    def _():
        finish()


def _attn_call(qt, k, vt):
    bsz, nt = k.shape[0], k.shape[1]
    lp = nt * ROW_TILE
    stored = lambda s: (s + nt - 1) % nt
    return pl.pallas_call(
        _attn_kernel,
        grid=(bsz, nt),
        in_specs=[pl.BlockSpec((1, QK_W, ROW_TILE), lambda b, s: (b, 0, stored(s))),
                  pl.BlockSpec((1, nt, ROW_TILE, QK_W), lambda b, s: (b, 0, 0, 0)),
                  pl.BlockSpec((1, nt, MLA_W, ROW_TILE), lambda b, s: (b, 0, 0, 0))],
        out_specs=pl.BlockSpec((1, ROW_TILE, MLA_W), lambda b, s: (b, stored(s), 0)),
        out_shape=jax.ShapeDtypeStruct((bsz, lp, MLA_W), BF16),
        scratch_shapes=[pltpu.VMEM((MLA_HEADS, V_DIM + SUM_ROWS, ROW_TILE), F32),
                        pltpu.VMEM((MLA_HEADS, ROW_TILE, ROW_TILE), F32),
                        pltpu.VMEM((MLA_HEADS, ROW_TILE, ROW_TILE), F32)],
        compiler_params=pltpu.CompilerParams(dimension_semantics=("arbitrary", "arbitrary"),
                                             vmem_limit_bytes=VMEM_LIMIT),
        name="mla_attention",
    )(qt, k, vt)


def _post_kernel(x_ref, u_ref, o_ref, smg_ref, png_ref, wg_ref, gb_ref, wpool_ref, wmla_ref, wconf_ref,
                 wsc_ref, wo_ref, pog_ref, out_ref):
    x = x_ref[0]
    h = _rms(x, png_ref[...]).astype(BF16)
    u = u_ref[0]
    ub = (o_ref[0].astype(F32) * smg_ref[0].astype(F32)).astype(BF16)
    branch_in = (u[:, 0:POOL_W], ub, u[:, POOL_W:POOL_W + CONF_W], u[:, POOL_W + CONF_W:])
    branch_w = (wpool_ref, wmla_ref, wconf_ref, wsc_ref)
    m = None
    for br in range(N_BRANCH):
        csl = slice(br * D_MODEL, (br + 1) * D_MODEL)
        gate = jax.nn.sigmoid(_dot(h, wg_ref[:, csl]) + gb_ref[:, csl])
        y = gate * _dot(branch_in[br], branch_w[br][...])
        m = y if m is None else m + y
    mo = _dot(m.astype(BF16), wo_ref[...])
    out_ref[0] = x + _rms(mo, pog_ref[...])


def _post_call(x, u, o, smg, layer, wts, rows_out):
    groups = x.shape[0]
    row_spec = lambda w: pl.BlockSpec((1, POST_TILE, w), lambda g, r: (g, r, 0))
    consts = [wts[n] for n in ("png", "wg", "gb", "wpool", "wmla", "wconf", "wsc", "wo", "pog")]
    return pl.pallas_call(
        _post_kernel,
        grid=(groups, rows_out // POST_TILE),
        in_specs=[row_spec(D_MODEL), row_spec(U_W), row_spec(MLA_W), row_spec(MLA_W)]
        + [_layer_spec(c, layer) for c in consts],
        out_specs=row_spec(D_MODEL),
        out_shape=jax.ShapeDtypeStruct((groups, rows_out, D_MODEL), F32),
        compiler_params=pltpu.CompilerParams(dimension_semantics=("arbitrary", "arbitrary"),
                                             vmem_limit_bytes=VMEM_LIMIT),
        name="mixer_post",
    )(x, u, o, smg, *consts)


def _rope_tables(seq):
    inv = 1.0 / (ROPE_THETA ** (jnp.arange(0, QK_ROPE, 2, dtype=F32) / QK_ROPE))
    pos = jnp.concatenate([N_META + jnp.arange(seq, dtype=F32), jnp.arange(ROW_TILE, dtype=F32)])
    ang = pos[:, None] * inv[None, :]
    cos, sin = jnp.cos(ang), jnp.sin(ang)
    n = pos.shape[0]
    zeros = lambda w: jnp.zeros((n, w), F32)
    tail = HEAD_PAD - QK_DIM
    ck = jnp.concatenate([zeros(QK_NOPE), cos, cos, zeros(tail)], axis=1)
    sk = jnp.concatenate([zeros(QK_NOPE), -sin, sin, zeros(tail)], axis=1)
    return (cos * Q_SCALE).T, (sin * Q_SCALE).T, ck, sk


def _swap_halves(w):
    half = QK_ROPE // 2
    return jnp.concatenate([w[..., half:], w[..., :half]], axis=-1)


def _prepare_weights(p):
    w_in = p["w_in"]
    cuts = [0] + [int(s) for s in np.cumsum(IN_SPLITS)]
    (w_pv, w_pg, w_cq, w_ckv, w_kr, w_mg, w_cu, w_cg, w_sb, w_sg, w_gl) = [
        w_in[:, :, a:b] for a, b in zip(cuts[:-1], cuts[1:])]
    kr_tile = jnp.concatenate([jnp.zeros((DEPTH, D_MODEL, QK_NOPE), F32), w_kr, _swap_halves(w_kr)], axis=-1)
    w1 = jnp.concatenate([w_pv, w_pg, w_cq, w_ckv, kr_tile, w_mg, w_cu, w_cg, w_sb, w_sg], axis=-1)

    pbd = jnp.zeros((DEPTH, POOL_W, POOL_W), F32)
    for g in range(len(POOL_WINDOWS)):
        sl = slice(g * POOL_GROUP, (g + 1) * POOL_GROUP)
        pbd = pbd.at[:, sl, sl].set(p["pool_w"][:, g])

    wq_t = jnp.swapaxes(p["w_uq"], 1, 2)
    wukv = p["w_ukv"].reshape(DEPTH, KV_RANK, MLA_HEADS, QK_NOPE + V_DIM)
    wk = jnp.concatenate([wukv[..., :QK_NOPE], jnp.zeros(wukv.shape[:3] + (HEAD_PAD - QK_NOPE,), F32)],
                         axis=-1).reshape(DEPTH, KV_RANK, QK_W)
    wv_t = jnp.swapaxes(wukv[..., QK_NOPE:].reshape(DEPTH, KV_RANK, MLA_W), 1, 2)
    row = lambda a: a[:, None, :]
    bf = lambda a: a.astype(BF16)
    return dict(
        png=row(p["pre_norm_g"]), w1=bf(w1), pbd=bf(pbd), psc=row(p["pool_scale"]), qng=row(p["q_norm_g"]),
        wq=bf(wq_t), kvg=row(p["kv_norm_g"]), wk=bf(wk), wv=bf(wv_t), cw=p["conf_dw_w"],
        cb=row(p["conf_dw_b"]), lg=row(p["conf_ln_g"]), lb=row(p["conf_ln_b"]), sw=p["sc_dw_w"],
        wg=bf(w_gl), gb=row(p["gate_bias"]), wpool=bf(p["w_out_pool"]), wmla=bf(p["w_out_mla"]),
        wconf=bf(p["w_out_conf"]), wsc=bf(p["w_out_sc"]), wo=bf(p["w_o"]), pog=row(p["post_norm_g"]))


def kernel(x, meta_tokens, pre_norm_g, w_in, gate_bias, pool_w, pool_scale, w_out_pool, q_norm_g, w_uq,
           kv_norm_g, w_ukv, w_out_mla, conf_dw_w, conf_dw_b, conf_ln_g, conf_ln_b, w_out_conf, sc_dw_w,
           w_out_sc, w_o, post_norm_g):
    bsz, seq, _ = x.shape
    assert seq % POST_TILE == 0 and meta_tokens.shape[0] == N_META
    n_seq_tiles = seq // ROW_TILE
    lp = seq + ROW_TILE
    assert (bsz * lp) % POST_TILE == 0
    wts = _prepare_weights(dict(
        w_in=w_in, pre_norm_g=pre_norm_g, gate_bias=gate_bias, pool_w=pool_w, pool_scale=pool_scale,
        w_out_pool=w_out_pool, q_norm_g=q_norm_g, w_uq=w_uq, kv_norm_g=kv_norm_g, w_ukv=w_ukv,
        w_out_mla=w_out_mla, conf_dw_w=conf_dw_w, conf_dw_b=conf_dw_b, conf_ln_g=conf_ln_g,
        conf_ln_b=conf_ln_b, w_out_conf=w_out_conf, sc_dw_w=sc_dw_w, w_out_sc=w_out_sc, w_o=w_o,
        post_norm_g=post_norm_g))
    tabs = _rope_tables(seq)
    meta_tile = jnp.concatenate([meta_tokens.astype(x.dtype),
                                 jnp.zeros((ROW_TILE - N_META, D_MODEL), x.dtype)])[None]

    h_res = None
    for layer in range(DEPTH):
        if layer == 0:
            qt, k, vt, smg, u, h_res = _pre_call(x, meta_tile, 0, layer, wts, tabs, bsz, n_seq_tiles, True)
        else:
            qt, k, vt, smg, u = _pre_call(h_res, h_res, n_seq_tiles, layer, wts, tabs, bsz, n_seq_tiles,
                                          False)
        o = _attn_call(qt, k, vt)
        if layer < DEPTH - 1:
            flat = lambda a: a.reshape(1, bsz * lp, a.shape[-1])
            h_res = _post_call(flat(h_res), flat(u), flat(o), flat(smg), layer, wts,
                               bsz * lp).reshape(bsz, lp, D_MODEL)
        else:
            h_res = _post_call(h_res, u, o, smg, layer, wts, seq)
    return h_res
```

```python
import functools

import jax
import jax.numpy as jnp
import numpy as np
from jax import lax
from jax.experimental import pallas as pl
from jax.experimental.pallas import tpu as pltpu

F32 = jnp.float32
BF16 = jnp.bfloat16

D_MODEL = 1024
DEPTH = 4
N_META = 16
EPS = 1e-6
N_BRANCH = 4
POOL_WINDOWS = (2, 4, 8, 16)
POOL_GROUP = 64
POOL_W = 256
MLA_HEADS = 8
QK_NOPE = 64
QK_ROPE = 32
QK_DIM = QK_NOPE + QK_ROPE
V_DIM = 64
Q_RANK = 256
KV_RANK = 128
ROPE_THETA = 10000.0
MLA_W = MLA_HEADS * V_DIM
CONF_W = 256
CONF_K = 31
SC_W = 256
SC_K = 3
IN_SPLITS = (POOL_W, POOL_W, Q_RANK, KV_RANK, QK_ROPE, MLA_W, 2 * CONF_W, CONF_W, 3 * SC_W, SC_W,
             N_BRANCH * D_MODEL)
U_W = POOL_W + CONF_W + SC_W

LANES = 128
SUBLANES = 8
HEAD_PAD = LANES
QK_W = MLA_HEADS * HEAD_PAD
ROW_TILE = 256
POST_TILE = 4 * ROW_TILE
POOL_HALO = 16
CONF_HALO = 32
SC_HALO = 8
SUM_ROWS = 16
QK_LEAD = 3
META_KEYS = LANES
VMEM_LIMIT = 56 * 1024 * 1024
LOG2_E = 1.4426950408889634
Q_SCALE = QK_DIM ** -0.5 * LOG2_E

C_PV, C_PG, C_CQ, C_CKV, C_KR, C_MG, C_CU, C_CG, C_SB, C_SG, C_END = (
    0, 256, 512, 768, 896, 1024, 1536, 2048, 2304, 3072, 3328)


def _rms(x, g):
    return x * lax.rsqrt(jnp.mean(x * x, axis=-1, keepdims=True) + EPS) * g


def _dot(a, b):
    return jnp.dot(a, b, preferred_element_type=F32)


def _dot_nt(a, b):
    return lax.dot_general(a, b, (((1,), (1,)), ((), ())), preferred_element_type=F32)


def _shifted_rows(buf, shifted, off, rows, lanes):
    tile, sh = divmod(off, SUBLANES)
    if sh == 0:
        return buf[off:off + rows, lanes]
    return shifted[sh - 1, tile * SUBLANES:tile * SUBLANES + rows, lanes]


def _pre_kernel(first, *refs):
    if first:
        x_ref, xm_ref, png_ref = refs[:3]
        refs = refs[3:]
    else:
        h_ref = refs[0]
        refs = refs[1:]
    (w1_ref, pbd_ref, psc_ref, qng_ref, wq_ref, kvg_ref, wk_ref, wv_ref, qc_ref, qs_ref, ck_ref, sk_ref,
     cw_ref, cb_ref, lg_ref, lb_ref, sw_ref, q_ref, k_ref, v_ref, smg_ref, u_ref) = refs[:22]
    refs = refs[22:]
    if first:
        xo_ref, ho_ref = refs[:2]
        refs = refs[2:]
    pbuf, cbuf, sbuf, pshift, cshift = refs
    tm = ROW_TILE
    t = pl.program_id(1)
    is_meta = t == 0

    @pl.when(is_meta)
    def _():
        pbuf[0:POOL_HALO, :] = jnp.zeros((POOL_HALO, POOL_W), F32)
        cbuf[0:CONF_HALO, :] = jnp.zeros((CONF_HALO, CONF_W), F32)
        sbuf[0:SC_HALO, :] = jnp.zeros((SC_HALO, SC_W), F32)

    if first:
        x = jnp.where(is_meta, xm_ref[0], x_ref[0])
        xo_ref[0] = x
        ho_ref[0] = _rms(x, png_ref[...]).astype(BF16)
        h_ref = ho_ref
    valid = pl.multiple_of(jnp.where(is_meta, N_META, tm), SUBLANES)

    def proj(lo, hi):
        return _dot(h_ref[0], w1_ref[:, lo:hi])

    zu = proj(C_CU, C_SB)
    glu = zu[:, 0:CONF_W] * jax.nn.sigmoid(zu[:, CONF_W:2 * CONF_W])
    cbuf[CONF_HALO:CONF_HALO + tm, :] = glu
    for sh in range(1, SUBLANES):
        cshift[sh - 1] = cbuf[sh:sh + CONF_HALO + tm - SUBLANES, :]

    def conv_taps(acc, k_lo, k_hi):
        base = CONF_HALO - (CONF_K - 1)
        for kk in range(k_lo, k_hi):
            acc = acc + cw_ref[kk:kk + 1, :] * _shifted_rows(cbuf, cshift, base + kk, tm, slice(None))
        return acc

    tap_cuts = (0, 8, 16, 24, CONF_K)
    acc = conv_taps(jnp.zeros((tm, CONF_W), F32) + cb_ref[...], tap_cuts[0], tap_cuts[1])
    zp = proj(C_PV, C_CQ)
    acc = conv_taps(acc, tap_cuts[1], tap_cuts[2])
    zc = proj(C_CQ, C_MG)
    cqn = _rms(zc[:, 0:Q_RANK], qng_ref[...]).astype(BF16)
    ckvn = _rms(zc[:, Q_RANK:Q_RANK + KV_RANK], kvg_ref[...]).astype(BF16)
    zs = proj(C_SB, C_END)
    acc = conv_taps(acc, tap_cuts[2], tap_cuts[3])
    qa = _dot_nt(wq_ref[...], cqn)
    kn = _dot(ckvn, wk_ref[...])
    vt = _dot_nt(wv_ref[...], ckvn)
    zm = proj(C_MG, C_CU)
    acc = conv_taps(acc, tap_cuts[3], tap_cuts[4])
    mu = jnp.mean(acc, axis=-1, keepdims=True)
    cen = acc - mu
    var = jnp.mean(cen * cen, axis=-1, keepdims=True)
    yc = cen * lax.rsqrt(var + EPS) * lg_ref[...] + lb_ref[...]
    yc = jax.nn.silu(yc) * jax.nn.silu(zu[:, 2 * CONF_W:3 * CONF_W])
    u_ref[0, :, POOL_W:POOL_W + CONF_W] = yc.astype(BF16)
    cbuf[0:CONF_HALO, :] = cbuf[pl.ds(valid, CONF_HALO), :]

    pv = zp[:, :POOL_W]
    pbuf[POOL_HALO:POOL_HALO + tm, :] = pv
    first_pos = jnp.where(is_meta, 0, N_META + (t - 1) * tm)
    pos = lax.broadcasted_iota(jnp.int32, (tm, LANES), 0) + first_pos
    low_half = lax.broadcasted_iota(jnp.int32, (tm, LANES), 1) < POOL_GROUP

    def window_sum(col, acc, j_lo, j_hi):
        for j in range(j_lo, j_hi):
            acc = acc + _shifted_rows(pbuf, pshift, POOL_HALO - j, tm, slice(col, col + LANES))
        return acc

    pooled = []
    for tile, (w_lo, w_hi) in enumerate(((POOL_WINDOWS[0], POOL_WINDOWS[1]),
                                         (POOL_WINDOWS[2], POOL_WINDOWS[3]))):
        col = tile * LANES
        lanes = slice(col, col + LANES)
        for sh in sorted({(POOL_HALO - j) % SUBLANES for j in range(1, w_hi)} - {0}):
            pshift[sh - 1, :, lanes] = pbuf[sh:sh + POOL_HALO + tm - SUBLANES, lanes]
        self_v = pv[:, col:col + LANES]
        s_lo = window_sum(col, self_v, 1, w_lo)
        s_hi = window_sum(col, s_lo, w_lo, w_hi)
        width = jnp.where(low_half, w_lo, w_hi)
        cnt = jnp.minimum(pos + 1, width).astype(F32)
        pooled.append(jnp.where(low_half, s_lo, s_hi) / cnt - self_v)
    p = jnp.concatenate(pooled, axis=1).astype(BF16)
    ya = _dot(p, pbd_ref[...]) * psc_ref[...] * jax.nn.silu(zp[:, POOL_W:])
    u_ref[0, :, 0:POOL_W] = ya.astype(BF16)
    pbuf[0:POOL_HALO, :] = pbuf[pl.ds(valid, POOL_HALO), :]

    qc = qc_ref[...]
    qs = qs_ref[...]
    half = QK_ROPE // 2
    for hd in range(MLA_HEADS):
        src = hd * QK_DIM
        dst = hd * HEAD_PAD
        t1 = qa[src + QK_NOPE:src + QK_NOPE + half, :]
        t2 = qa[src + QK_NOPE + half:src + QK_DIM, :]
        q_ref[0, dst:dst + QK_NOPE, :] = (qa[src:src + QK_NOPE, :] * Q_SCALE).astype(BF16)
        q_ref[0, dst + QK_NOPE:dst + QK_NOPE + half, :] = (t1 * qc - t2 * qs).astype(BF16)
        q_ref[0, dst + QK_NOPE + half:dst + QK_DIM, :] = (t1 * qs + t2 * qc).astype(BF16)
        q_ref[0, dst + QK_DIM:dst + HEAD_PAD, :] = jnp.zeros((HEAD_PAD - QK_DIM, tm), BF16)
    zk = zc[:, C_KR - C_CQ:C_MG - C_CQ]
    k_rope = zk * ck_ref[...] + pltpu.roll(zk, LANES - QK_ROPE, 1) * sk_ref[...]
    for hd in range(MLA_HEADS):
        sl = slice(hd * HEAD_PAD, (hd + 1) * HEAD_PAD)
        k_ref[0, 0, :, sl] = (kn[:, sl] + k_rope).astype(BF16)
    v_ref[0, 0] = vt.astype(BF16)
    smg_ref[0] = jax.nn.silu(zm).astype(BF16)

    sbuf[SC_HALO:SC_HALO + tm, :] = zs[:, SC_W:2 * SC_W] * zs[:, 2 * SC_W:3 * SC_W]
    conv = jnp.zeros((tm, SC_W), F32)
    base = SC_HALO - (SC_K - 1)
    for kk in range(SC_K):
        conv = conv + sw_ref[kk:kk + 1, :] * sbuf[base + kk:base + kk + tm, :]
    yd = zs[:, 0:SC_W] * conv * jax.nn.silu(zs[:, 3 * SC_W:4 * SC_W])
    u_ref[0, :, POOL_W + CONF_W:U_W] = yd.astype(BF16)
    sbuf[0:SC_HALO, :] = sbuf[pl.ds(valid, SC_HALO), :]


def _layer_spec(arr, layer, single_buffer=False):
    mode = pl.Buffered(1) if single_buffer else None
    return pl.BlockSpec((None,) + arr.shape[1:], lambda *_: (layer, 0, 0), pipeline_mode=mode)


def _pre_call(first, acts, layer, wts, tabs, bsz, n_seq_tiles):
    nt = n_seq_tiles + 1
    lp = nt * ROW_TILE
    stored = lambda t: (t + n_seq_tiles) % nt
    row_spec = lambda w: pl.BlockSpec((1, ROW_TILE, w), lambda b, t: (b, stored(t), 0))
    consts = [wts[n] for n in ("w1", "pbd", "psc", "qng", "wq", "kvg", "wk", "wv")]
    consts2 = [wts[n] for n in ("cw", "cb", "lg", "lb", "sw")]
    if first:
        consts = [wts["png"]] + consts
        act_specs = [pl.BlockSpec((1, ROW_TILE, D_MODEL), lambda b, t: (b, jnp.maximum(t - 1, 0), 0)),
                     pl.BlockSpec((1, ROW_TILE, D_MODEL), lambda b, t: (0, 0, 0))]
    else:
        act_specs = [row_spec(D_MODEL)]
    in_specs = (act_specs + [_layer_spec(c, layer) for c in consts]
                + [pl.BlockSpec((QK_ROPE // 2, ROW_TILE), lambda b, t: (0, stored(t)))] * 2
                + [pl.BlockSpec((ROW_TILE, LANES), lambda b, t: (stored(t), 0))] * 2
                + [_layer_spec(c, layer) for c in consts2])
    out_specs = [pl.BlockSpec((1, QK_W, ROW_TILE), lambda b, t: (b, 0, stored(t))),
                 pl.BlockSpec((1, 1, ROW_TILE, QK_W), lambda b, t: (b, stored(t), 0, 0)),
                 pl.BlockSpec((1, 1, MLA_W, ROW_TILE), lambda b, t: (b, stored(t), 0, 0)),
                 row_spec(MLA_W), row_spec(U_W)]
    out_shape = [jax.ShapeDtypeStruct((bsz, QK_W, lp), BF16),
                 jax.ShapeDtypeStruct((bsz, nt, ROW_TILE, QK_W), BF16),
                 jax.ShapeDtypeStruct((bsz, nt, MLA_W, ROW_TILE), BF16),
                 jax.ShapeDtypeStruct((bsz, lp, MLA_W), BF16),
                 jax.ShapeDtypeStruct((bsz, lp, U_W), BF16)]
    if first:
        out_specs += [row_spec(D_MODEL), row_spec(D_MODEL)]
        out_shape += [jax.ShapeDtypeStruct((bsz, lp, D_MODEL), F32),
                      jax.ShapeDtypeStruct((bsz, lp, D_MODEL), BF16)]
    return pl.pallas_call(
        functools.partial(_pre_kernel, first),
        grid=(bsz, nt),
        in_specs=in_specs,
        out_specs=out_specs,
        out_shape=out_shape,
        scratch_shapes=[pltpu.VMEM((POOL_HALO + ROW_TILE, POOL_W), F32),
                        pltpu.VMEM((CONF_HALO + ROW_TILE, CONF_W), F32),
                        pltpu.VMEM((SC_HALO + ROW_TILE, SC_W), F32),
                        pltpu.VMEM((SUBLANES - 1, POOL_HALO + ROW_TILE - SUBLANES, POOL_W), F32),
                        pltpu.VMEM((SUBLANES - 1, CONF_HALO + ROW_TILE - SUBLANES, CONF_W), F32)],
        compiler_params=pltpu.CompilerParams(dimension_semantics=("arbitrary", "arbitrary"),
                                             vmem_limit_bytes=VMEM_LIMIT),
        name="mixer_pre",
    )(*acts, *consts, *tabs, *consts2)


def _attn_kernel(qt_ref, k_ref, vt_ref, o_ref, acc_sc, s_even, s_odd):
    tq = ROW_TILE
    n_seq = k_ref.shape[1] - 1
    is_seq = pl.program_id(1) > 0
    i = pl.program_id(1) - 1
    key_idx = lax.broadcasted_iota(jnp.int32, (tq, tq), 0)
    qry_idx = lax.broadcasted_iota(jnp.int32, (tq, tq), 1)
    causal = key_idx <= qry_idx
    ones_rows = jnp.ones((SUM_ROWS, tq), BF16)

    def score_head(s_ref, j, hd):
        hsl = slice(hd * HEAD_PAD, (hd + 1) * HEAD_PAD)
        s_ref[hd] = _dot(k_ref[0, j, :, hsl], qt_ref[0, hsl, :])

    def scores_into(s_ref, j):
        for hd in range(MLA_HEADS):
            score_head(s_ref, j, hd)

    mkey = lax.broadcasted_iota(jnp.int32, (N_META, tq), 0)
    mqry = lax.broadcasted_iota(jnp.int32, (N_META, tq), 1)
    meta_ok = is_seq | (mkey <= mqry)
    meta_scores = []
    for hd in range(MLA_HEADS):
        hsl = slice(hd * HEAD_PAD, (hd + 1) * HEAD_PAD)
        meta_scores.append(_dot(k_ref[0, n_seq, 0:N_META, hsl], qt_ref[0, hsl, :]))
    scores_into(s_even, 0)
    zero_rows = jnp.zeros((META_KEYS - N_META, tq), BF16)
    maxes = []
    for hd in range(MLA_HEADS):
        vsl = slice(hd * V_DIM, (hd + 1) * V_DIM)
        s = jnp.where(meta_ok, meta_scores[hd], -jnp.inf)
        m0 = jnp.max(s, axis=0, keepdims=True)
        p = jnp.concatenate([jnp.exp2(s - m0).astype(BF16), zero_rows], axis=0)
        v_ext = jnp.concatenate([vt_ref[0, n_seq, vsl, 0:META_KEYS], ones_rows[:, 0:META_KEYS]], axis=0)
        acc_sc[hd] = _dot(v_ext, p)
        maxes.append(m0)
    maxes = tuple(maxes)

    def consume_head(s_ref, j, hd, m_old, masked):
        vsl = slice(hd * V_DIM, (hd + 1) * V_DIM)
        s = s_ref[hd]
        if masked:
            s = jnp.where(causal, s, -jnp.inf)
        m_new = jnp.maximum(m_old, jnp.max(s, axis=0, keepdims=True))
        alpha = jnp.exp2(m_old - m_new)
        p = jnp.exp2(s - m_new).astype(BF16)
        v_ext = jnp.concatenate([vt_ref[0, j, vsl, :], ones_rows], axis=0)
        acc_sc[hd] = alpha * acc_sc[hd] + _dot(v_ext, p)
        return m_new

    def consume(s_ref, j, maxes, masked, s_next=None, j_next=None):
        new_maxes = []
        if s_next is not None:
            for hd in range(QK_LEAD):
                score_head(s_next, j_next, hd)
        for hd in range(MLA_HEADS):
            new_maxes.append(consume_head(s_ref, j, hd, maxes[hd], masked))
            if s_next is not None and hd + QK_LEAD < MLA_HEADS:
                score_head(s_next, j_next, hd + QK_LEAD)
        return tuple(new_maxes)

    def finish():
        for pair in range(MLA_HEADS // 2):
            halves = []
            for hd in (2 * pair, 2 * pair + 1):
                acc = acc_sc[hd]
                halves.append(acc[0:V_DIM, :] / acc[V_DIM:V_DIM + 1, :])
            o_ref[0, :, pair * LANES:(pair + 1) * LANES] = jnp.concatenate(halves, axis=0).T.astype(BF16)

    def two_blocks(jj, maxes):
        j = 2 * jj
        maxes = consume(s_even, j, maxes, False, s_odd, j + 1)
        return consume(s_odd, j + 1, maxes, False, s_even, j + 2)

    def four_blocks(jj, maxes):
        return two_blocks(2 * jj + 1, two_blocks(2 * jj, maxes))

    n_full = jnp.where(is_seq, i, 0)
    maxes = lax.fori_loop(0, n_full // 4, four_blocks, maxes)
    maxes = lax.fori_loop(2 * (n_full // 4), n_full // 2, two_blocks, maxes)

    @pl.when(is_seq & (i % 2 == 0))
    def _():
        consume(s_even, i, maxes, True)
        finish()

    @pl.when(is_seq & (i % 2 == 1))
    def _():
        consume(s_odd, i, consume(s_even, i - 1, maxes, False, s_odd, i), True)
        finish()

    @pl.when(jnp.logical_not(is_seq))
    def _():
        finish()


def _attn_call(qt, k, vt):
    bsz, nt = k.shape[0], k.shape[1]
    lp = nt * ROW_TILE
    stored = lambda s: (s + nt - 1) % nt
    return pl.pallas_call(
        _attn_kernel,
        grid=(bsz, nt),
        in_specs=[pl.BlockSpec((1, QK_W, ROW_TILE), lambda b, s: (b, 0, stored(s))),
                  pl.BlockSpec((1, nt, ROW_TILE, QK_W), lambda b, s: (b, 0, 0, 0)),
                  pl.BlockSpec((1, nt, MLA_W, ROW_TILE), lambda b, s: (b, 0, 0, 0))],
        out_specs=pl.BlockSpec((1, ROW_TILE, MLA_W), lambda b, s: (b, stored(s), 0)),
        out_shape=jax.ShapeDtypeStruct((bsz, lp, MLA_W), BF16),
        scratch_shapes=[pltpu.VMEM((MLA_HEADS, V_DIM + SUM_ROWS, ROW_TILE), F32),
                        pltpu.VMEM((MLA_HEADS, ROW_TILE, ROW_TILE), F32),
                        pltpu.VMEM((MLA_HEADS, ROW_TILE, ROW_TILE), F32)],
        compiler_params=pltpu.CompilerParams(dimension_semantics=("arbitrary", "arbitrary"),
                                             vmem_limit_bytes=VMEM_LIMIT),
        name="mla_attention",
    )(qt, k, vt)


def _post_kernel(emit_h, x_ref, h_ref, u_ref, o_ref, smg_ref, wg_ref, gb_ref, wpool_ref, wmla_ref, wconf_ref,
                 wsc_ref, wo_ref, pog_ref, *rest):
    if emit_h:
        png_next_ref, out_ref, ho_ref = rest
    else:
        (out_ref,) = rest
    x = x_ref[0]
    h = h_ref[0]
    u = u_ref[0]
    ub = (o_ref[0].astype(F32) * smg_ref[0].astype(F32)).astype(BF16)
    branch_in = (u[:, 0:POOL_W], ub, u[:, POOL_W:POOL_W + CONF_W], u[:, POOL_W + CONF_W:])
    branch_w = (wpool_ref, wmla_ref, wconf_ref, wsc_ref)
    m = None
    for br in range(N_BRANCH):
        csl = slice(br * D_MODEL, (br + 1) * D_MODEL)
        gate = jax.nn.sigmoid(_dot(h, wg_ref[:, csl]) + gb_ref[:, csl])
        y = gate * _dot(branch_in[br], branch_w[br][...])
        m = y if m is None else m + y
    mo = _dot(m.astype(BF16), wo_ref[...])
    x_new = x + _rms(mo, pog_ref[...])
    out_ref[0] = x_new
    if emit_h:
        ho_ref[0] = _rms(x_new, png_next_ref[...]).astype(BF16)


def _post_call(x, h, u, o, smg, layer, wts, rows_out, emit_h):
    groups = x.shape[0]
    row_spec = lambda w: pl.BlockSpec((1, POST_TILE, w), lambda g, r: (g, r, 0))
    consts = [wts[n] for n in ("wg", "gb", "wpool", "wmla", "wconf", "wsc", "wo", "pog")]
    const_specs = [_layer_spec(c, layer, single_buffer=True) for c in consts]
    out_specs = [row_spec(D_MODEL)]
    out_shape = [jax.ShapeDtypeStruct((groups, rows_out, D_MODEL), F32)]
    if emit_h:
        consts.append(wts["png"])
        const_specs.append(_layer_spec(wts["png"], layer + 1, single_buffer=True))
        out_specs.append(row_spec(D_MODEL))
        out_shape.append(jax.ShapeDtypeStruct((groups, rows_out, D_MODEL), BF16))
    return pl.pallas_call(
        functools.partial(_post_kernel, emit_h),
        grid=(groups, rows_out // POST_TILE),
        in_specs=[row_spec(D_MODEL), row_spec(D_MODEL), row_spec(U_W), row_spec(MLA_W), row_spec(MLA_W)]
        + const_specs,
        out_specs=out_specs,
        out_shape=out_shape,
        compiler_params=pltpu.CompilerParams(dimension_semantics=("arbitrary", "arbitrary"),
                                             vmem_limit_bytes=VMEM_LIMIT),
        name="mixer_post",
    )(x, h, u, o, smg, *consts)


def _rope_tables(seq):
    inv = 1.0 / (ROPE_THETA ** (jnp.arange(0, QK_ROPE, 2, dtype=F32) / QK_ROPE))
    pos = jnp.concatenate([N_META + jnp.arange(seq, dtype=F32), jnp.arange(ROW_TILE, dtype=F32)])
    ang = pos[:, None] * inv[None, :]
    cos, sin = jnp.cos(ang), jnp.sin(ang)
    n = pos.shape[0]
    zeros = lambda w: jnp.zeros((n, w), F32)
    tail = HEAD_PAD - QK_DIM
    ck = jnp.concatenate([zeros(QK_NOPE), cos, cos, zeros(tail)], axis=1)
    sk = jnp.concatenate([zeros(QK_NOPE), -sin, sin, zeros(tail)], axis=1)
    return (cos * Q_SCALE).T, (sin * Q_SCALE).T, ck, sk


def _swap_halves(w):
    half = QK_ROPE // 2
    return jnp.concatenate([w[..., half:], w[..., :half]], axis=-1)


def _prepare_weights(p):
    w_in = p["w_in"]
    cuts = [0] + [int(s) for s in np.cumsum(IN_SPLITS)]
    (w_pv, w_pg, w_cq, w_ckv, w_kr, w_mg, w_cu, w_cg, w_sb, w_sg, w_gl) = [
        w_in[:, :, a:b] for a, b in zip(cuts[:-1], cuts[1:])]
    kr_tile = jnp.concatenate([jnp.zeros((DEPTH, D_MODEL, QK_NOPE), F32), w_kr, _swap_halves(w_kr)], axis=-1)
    w1 = jnp.concatenate([w_pv, w_pg, w_cq, w_ckv, kr_tile, w_mg, w_cu, w_cg, w_sb, w_sg], axis=-1)

    pbd = jnp.zeros((DEPTH, POOL_W, POOL_W), F32)
    for g in range(len(POOL_WINDOWS)):
        sl = slice(g * POOL_GROUP, (g + 1) * POOL_GROUP)
        pbd = pbd.at[:, sl, sl].set(p["pool_w"][:, g])

    wq_t = jnp.swapaxes(p["w_uq"], 1, 2)
    wukv = p["w_ukv"].reshape(DEPTH, KV_RANK, MLA_HEADS, QK_NOPE + V_DIM)
    wk = jnp.concatenate([wukv[..., :QK_NOPE], jnp.zeros(wukv.shape[:3] + (HEAD_PAD - QK_NOPE,), F32)],
                         axis=-1).reshape(DEPTH, KV_RANK, QK_W)
    wv_t = jnp.swapaxes(wukv[..., QK_NOPE:].reshape(DEPTH, KV_RANK, MLA_W), 1, 2)
    row = lambda a: a[:, None, :]
    bf = lambda a: a.astype(BF16)
    return dict(
        png=row(p["pre_norm_g"]), w1=bf(w1), pbd=bf(pbd), psc=row(p["pool_scale"]), qng=row(p["q_norm_g"]),
        wq=bf(wq_t), kvg=row(p["kv_norm_g"]), wk=bf(wk), wv=bf(wv_t), cw=p["conf_dw_w"],
        cb=row(p["conf_dw_b"]), lg=row(p["conf_ln_g"]), lb=row(p["conf_ln_b"]), sw=p["sc_dw_w"],
        wg=bf(w_gl), gb=row(p["gate_bias"]), wpool=bf(p["w_out_pool"]), wmla=bf(p["w_out_mla"]),
        wconf=bf(p["w_out_conf"]), wsc=bf(p["w_out_sc"]), wo=bf(p["w_o"]), pog=row(p["post_norm_g"]))


def kernel(x, meta_tokens, pre_norm_g, w_in, gate_bias, pool_w, pool_scale, w_out_pool, q_norm_g, w_uq,
           kv_norm_g, w_ukv, w_out_mla, conf_dw_w, conf_dw_b, conf_ln_g, conf_ln_b, w_out_conf, sc_dw_w,
           w_out_sc, w_o, post_norm_g):
    bsz, seq, _ = x.shape
    assert seq % POST_TILE == 0 and meta_tokens.shape[0] == N_META
    n_seq_tiles = seq // ROW_TILE
    lp = seq + ROW_TILE
    assert (bsz * lp) % POST_TILE == 0
    wts = _prepare_weights(dict(
        w_in=w_in, pre_norm_g=pre_norm_g, gate_bias=gate_bias, pool_w=pool_w, pool_scale=pool_scale,
        w_out_pool=w_out_pool, q_norm_g=q_norm_g, w_uq=w_uq, kv_norm_g=kv_norm_g, w_ukv=w_ukv,
        w_out_mla=w_out_mla, conf_dw_w=conf_dw_w, conf_dw_b=conf_dw_b, conf_ln_g=conf_ln_g,
        conf_ln_b=conf_ln_b, w_out_conf=w_out_conf, sc_dw_w=sc_dw_w, w_out_sc=w_out_sc, w_o=w_o,
        post_norm_g=post_norm_g))
    tabs = _rope_tables(seq)
    meta_tile = jnp.concatenate([meta_tokens.astype(x.dtype),
                                 jnp.zeros((ROW_TILE - N_META, D_MODEL), x.dtype)])[None]

    x_res = h_norm = None
    for layer in range(DEPTH):
        if layer == 0:
            qt, k, vt, smg, u, x_res, h_norm = _pre_call(True, (x, meta_tile), layer, wts, tabs, bsz,
                                                         n_seq_tiles)
        else:
            qt, k, vt, smg, u = _pre_call(False, (h_norm,), layer, wts, tabs, bsz, n_seq_tiles)
        o = _attn_call(qt, k, vt)
        if layer < DEPTH - 1:
            flat = lambda a: a.reshape(1, bsz * lp, a.shape[-1])
            x_res, h_norm = _post_call(flat(x_res), flat(h_norm), flat(u), flat(o), flat(smg), layer, wts,
                                       bsz * lp, True)
            x_res, h_norm = x_res.reshape(bsz, lp, D_MODEL), h_norm.reshape(bsz, lp, D_MODEL)
        else:
            (x_res,) = _post_call(x_res, h_norm, u, o, smg, layer, wts, seq, False)
    return x_res
```

```python
import functools

import jax
import jax.numpy as jnp
import numpy as np
from jax import lax
from jax.experimental import pallas as pl
from jax.experimental.pallas import tpu as pltpu

F32 = jnp.float32
BF16 = jnp.bfloat16

D_MODEL = 1024
DEPTH = 4
N_META = 16
EPS = 1e-6
N_BRANCH = 4
POOL_WINDOWS = (2, 4, 8, 16)
POOL_GROUP = 64
POOL_W = 256
MLA_HEADS = 8
QK_NOPE = 64
QK_ROPE = 32
QK_DIM = QK_NOPE + QK_ROPE
V_DIM = 64
Q_RANK = 256
KV_RANK = 128
ROPE_THETA = 10000.0
MLA_W = MLA_HEADS * V_DIM
CONF_W = 256
CONF_K = 31
SC_W = 256
SC_K = 3
IN_SPLITS = (POOL_W, POOL_W, Q_RANK, KV_RANK, QK_ROPE, MLA_W, 2 * CONF_W, CONF_W, 3 * SC_W, SC_W,
             N_BRANCH * D_MODEL)
U_W = POOL_W + CONF_W + SC_W

LANES = 128
SUBLANES = 8
HEAD_PAD = LANES
QK_W = MLA_HEADS * HEAD_PAD
ROW_TILE = 256
POST_TILE = 2 * ROW_TILE
POST_SPLIT = 2
POOL_HALO = 16
CONF_HALO = 32
SC_HALO = 8
SUM_ROWS = 16
QK_LEAD = 3
META_KEYS = LANES
VMEM_LIMIT = 56 * 1024 * 1024
LOG2_E = 1.4426950408889634
Q_SCALE = QK_DIM ** -0.5 * LOG2_E

C_PV, C_PG, C_CQ, C_CKV, C_KR, C_MG, C_CU, C_CG, C_SB, C_SG, C_END = (
    0, 256, 512, 768, 896, 1024, 1536, 2048, 2304, 3072, 3328)


def _rms(x, g):
    return x * lax.rsqrt(jnp.mean(x * x, axis=-1, keepdims=True) + EPS) * g


def _dot(a, b):
    return jnp.dot(a, b, preferred_element_type=F32)


def _dot_nt(a, b):
    return lax.dot_general(a, b, (((1,), (1,)), ((), ())), preferred_element_type=F32)


def _shifted_rows(buf, shifted, off, rows, lanes):
    tile, sh = divmod(off, SUBLANES)
    if sh == 0:
        return buf[off:off + rows, lanes]
    return shifted[sh - 1, tile * SUBLANES:tile * SUBLANES + rows, lanes]


def _pre_kernel(first, *refs):
    if first:
        x_ref, xm_ref, png_ref = refs[:3]
        refs = refs[3:]
    else:
        h_ref = refs[0]
        refs = refs[1:]
    (w1_ref, pbd_ref, psc_ref, qng_ref, wq_ref, kvg_ref, wk_ref, wv_ref, qc_ref, qs_ref, ck_ref, sk_ref,
     cw_ref, cb_ref, lg_ref, lb_ref, sw_ref, q_ref, k_ref, v_ref, smg_ref, u_ref) = refs[:22]
    refs = refs[22:]
    if first:
        xo_ref, ho_ref = refs[:2]
        refs = refs[2:]
    pbuf, cbuf, sbuf, pshift, cshift = refs
    tm = ROW_TILE
    t = pl.program_id(1)
    is_meta = t == 0

    @pl.when(is_meta)
    def _():
        pbuf[0:POOL_HALO, :] = jnp.zeros((POOL_HALO, POOL_W), F32)
        cbuf[0:CONF_HALO, :] = jnp.zeros((CONF_HALO, CONF_W), F32)
        sbuf[0:SC_HALO, :] = jnp.zeros((SC_HALO, SC_W), F32)

    if first:
        x = jnp.where(is_meta, xm_ref[0], x_ref[0])
        xo_ref[0] = x
        ho_ref[0] = _rms(x, png_ref[...]).astype(BF16)
        h_ref = ho_ref
    valid = pl.multiple_of(jnp.where(is_meta, N_META, tm), SUBLANES)

    def proj(lo, hi):
        return _dot(h_ref[0], w1_ref[:, lo:hi])

    zu = proj(C_CU, C_SB)
    glu = zu[:, 0:CONF_W] * jax.nn.sigmoid(zu[:, CONF_W:2 * CONF_W])
    cbuf[CONF_HALO:CONF_HALO + tm, :] = glu
    for sh in range(1, SUBLANES):
        cshift[sh - 1] = cbuf[sh:sh + CONF_HALO + tm - SUBLANES, :]

    def conv_taps(acc, k_lo, k_hi):
        base = CONF_HALO - (CONF_K - 1)
        for kk in range(k_lo, k_hi):
            acc = acc + cw_ref[kk:kk + 1, :] * _shifted_rows(cbuf, cshift, base + kk, tm, slice(None))
        return acc

    tap_cuts = (0, 8, 16, 24, CONF_K)
    acc = conv_taps(jnp.zeros((tm, CONF_W), F32) + cb_ref[...], tap_cuts[0], tap_cuts[1])
    zp = proj(C_PV, C_CQ)
    acc = conv_taps(acc, tap_cuts[1], tap_cuts[2])
    zc = proj(C_CQ, C_MG)
    cqn = _rms(zc[:, 0:Q_RANK], qng_ref[...]).astype(BF16)
    ckvn = _rms(zc[:, Q_RANK:Q_RANK + KV_RANK], kvg_ref[...]).astype(BF16)
    zs = proj(C_SB, C_END)
    acc = conv_taps(acc, tap_cuts[2], tap_cuts[3])
    qa = _dot_nt(wq_ref[...], cqn)
    kn = _dot(ckvn, wk_ref[...])
    vt = _dot_nt(wv_ref[...], ckvn)
    zm = proj(C_MG, C_CU)
    acc = conv_taps(acc, tap_cuts[3], tap_cuts[4])
    mu = jnp.mean(acc, axis=-1, keepdims=True)
    cen = acc - mu
    var = jnp.mean(cen * cen, axis=-1, keepdims=True)
    yc = cen * lax.rsqrt(var + EPS) * lg_ref[...] + lb_ref[...]
    yc = jax.nn.silu(yc) * jax.nn.silu(zu[:, 2 * CONF_W:3 * CONF_W])
    u_ref[0, :, POOL_W:POOL_W + CONF_W] = yc.astype(BF16)
    cbuf[0:CONF_HALO, :] = cbuf[pl.ds(valid, CONF_HALO), :]

    pv = zp[:, :POOL_W]
    pbuf[POOL_HALO:POOL_HALO + tm, :] = pv
    first_pos = jnp.where(is_meta, 0, N_META + (t - 1) * tm)
    pos = lax.broadcasted_iota(jnp.int32, (tm, LANES), 0) + first_pos
    low_half = lax.broadcasted_iota(jnp.int32, (tm, LANES), 1) < POOL_GROUP

    def window_sum(col, acc, j_lo, j_hi):
        for j in range(j_lo, j_hi):
            acc = acc + _shifted_rows(pbuf, pshift, POOL_HALO - j, tm, slice(col, col + LANES))
        return acc

    pooled = []
    for tile, (w_lo, w_hi) in enumerate(((POOL_WINDOWS[0], POOL_WINDOWS[1]),
                                         (POOL_WINDOWS[2], POOL_WINDOWS[3]))):
        col = tile * LANES
        lanes = slice(col, col + LANES)
        for sh in sorted({(POOL_HALO - j) % SUBLANES for j in range(1, w_hi)} - {0}):
            pshift[sh - 1, :, lanes] = pbuf[sh:sh + POOL_HALO + tm - SUBLANES, lanes]
        self_v = pv[:, col:col + LANES]
        s_lo = window_sum(col, self_v, 1, w_lo)
        s_hi = window_sum(col, s_lo, w_lo, w_hi)
        width = jnp.where(low_half, w_lo, w_hi)
        cnt = jnp.minimum(pos + 1, width).astype(F32)
        pooled.append(jnp.where(low_half, s_lo, s_hi) / cnt - self_v)
    p = jnp.concatenate(pooled, axis=1).astype(BF16)
    ya = _dot(p, pbd_ref[...]) * psc_ref[...] * jax.nn.silu(zp[:, POOL_W:])
    u_ref[0, :, 0:POOL_W] = ya.astype(BF16)
    pbuf[0:POOL_HALO, :] = pbuf[pl.ds(valid, POOL_HALO), :]

    qc = qc_ref[...]
    qs = qs_ref[...]
    half = QK_ROPE // 2
    for hd in range(MLA_HEADS):
        src = hd * QK_DIM
        dst = hd * HEAD_PAD
        t1 = qa[src + QK_NOPE:src + QK_NOPE + half, :]
        t2 = qa[src + QK_NOPE + half:src + QK_DIM, :]
        q_ref[0, dst:dst + QK_NOPE, :] = (qa[src:src + QK_NOPE, :] * Q_SCALE).astype(BF16)
        q_ref[0, dst + QK_NOPE:dst + QK_NOPE + half, :] = (t1 * qc - t2 * qs).astype(BF16)
        q_ref[0, dst + QK_NOPE + half:dst + QK_DIM, :] = (t1 * qs + t2 * qc).astype(BF16)
        q_ref[0, dst + QK_DIM:dst + HEAD_PAD, :] = jnp.zeros((HEAD_PAD - QK_DIM, tm), BF16)
    zk = zc[:, C_KR - C_CQ:C_MG - C_CQ]
    k_rope = zk * ck_ref[...] + pltpu.roll(zk, LANES - QK_ROPE, 1) * sk_ref[...]
    for hd in range(MLA_HEADS):
        sl = slice(hd * HEAD_PAD, (hd + 1) * HEAD_PAD)
        k_ref[0, 0, :, sl] = (kn[:, sl] + k_rope).astype(BF16)
    v_ref[0, 0] = vt.astype(BF16)
    smg_ref[0] = jax.nn.silu(zm).astype(BF16)

    sbuf[SC_HALO:SC_HALO + tm, :] = zs[:, SC_W:2 * SC_W] * zs[:, 2 * SC_W:3 * SC_W]
    conv = jnp.zeros((tm, SC_W), F32)
    base = SC_HALO - (SC_K - 1)
    for kk in range(SC_K):
        conv = conv + sw_ref[kk:kk + 1, :] * sbuf[base + kk:base + kk + tm, :]
    yd = zs[:, 0:SC_W] * conv * jax.nn.silu(zs[:, 3 * SC_W:4 * SC_W])
    u_ref[0, :, POOL_W + CONF_W:U_W] = yd.astype(BF16)
    sbuf[0:SC_HALO, :] = sbuf[pl.ds(valid, SC_HALO), :]


def _layer_spec(arr, layer, single_buffer=False):
    mode = pl.Buffered(1) if single_buffer else None
    return pl.BlockSpec((None,) + arr.shape[1:], lambda *_: (layer, 0, 0), pipeline_mode=mode)


def _pre_call(first, acts, layer, wts, tabs, bsz, n_seq_tiles):
    nt = n_seq_tiles + 1
    lp = nt * ROW_TILE
    stored = lambda t: (t + n_seq_tiles) % nt
    row_spec = lambda w: pl.BlockSpec((1, ROW_TILE, w), lambda b, t: (b, stored(t), 0))
    consts = [wts[n] for n in ("w1", "pbd", "psc", "qng", "wq", "kvg", "wk", "wv")]
    consts2 = [wts[n] for n in ("cw", "cb", "lg", "lb", "sw")]
    if first:
        consts = [wts["png"]] + consts
        act_specs = [pl.BlockSpec((1, ROW_TILE, D_MODEL), lambda b, t: (b, jnp.maximum(t - 1, 0), 0)),
                     pl.BlockSpec((1, ROW_TILE, D_MODEL), lambda b, t: (0, 0, 0))]
    else:
        act_specs = [row_spec(D_MODEL)]
    in_specs = (act_specs + [_layer_spec(c, layer) for c in consts]
                + [pl.BlockSpec((QK_ROPE // 2, ROW_TILE), lambda b, t: (0, stored(t)))] * 2
                + [pl.BlockSpec((ROW_TILE, LANES), lambda b, t: (stored(t), 0))] * 2
                + [_layer_spec(c, layer) for c in consts2])
    out_specs = [pl.BlockSpec((1, QK_W, ROW_TILE), lambda b, t: (b, 0, stored(t))),
                 pl.BlockSpec((1, 1, ROW_TILE, QK_W), lambda b, t: (b, stored(t), 0, 0)),
                 pl.BlockSpec((1, 1, MLA_W, ROW_TILE), lambda b, t: (b, stored(t), 0, 0)),
                 row_spec(MLA_W), row_spec(U_W)]
    out_shape = [jax.ShapeDtypeStruct((bsz, QK_W, lp), BF16),
                 jax.ShapeDtypeStruct((bsz, nt, ROW_TILE, QK_W), BF16),
                 jax.ShapeDtypeStruct((bsz, nt, MLA_W, ROW_TILE), BF16),
                 jax.ShapeDtypeStruct((bsz, lp, MLA_W), BF16),
                 jax.ShapeDtypeStruct((bsz, lp, U_W), BF16)]
    if first:
        out_specs += [row_spec(D_MODEL), row_spec(D_MODEL)]
        out_shape += [jax.ShapeDtypeStruct((bsz, lp, D_MODEL), F32),
                      jax.ShapeDtypeStruct((bsz, lp, D_MODEL), BF16)]
    return pl.pallas_call(
        functools.partial(_pre_kernel, first),
        grid=(bsz, nt),
        in_specs=in_specs,
        out_specs=out_specs,
        out_shape=out_shape,
        scratch_shapes=[pltpu.VMEM((POOL_HALO + ROW_TILE, POOL_W), F32),
                        pltpu.VMEM((CONF_HALO + ROW_TILE, CONF_W), F32),
                        pltpu.VMEM((SC_HALO + ROW_TILE, SC_W), F32),
                        pltpu.VMEM((SUBLANES - 1, POOL_HALO + ROW_TILE - SUBLANES, POOL_W), F32),
                        pltpu.VMEM((SUBLANES - 1, CONF_HALO + ROW_TILE - SUBLANES, CONF_W), F32)],
        compiler_params=pltpu.CompilerParams(dimension_semantics=("arbitrary", "arbitrary"),
                                             vmem_limit_bytes=VMEM_LIMIT),
        name="mixer_pre",
    )(*acts, *consts, *tabs, *consts2)


def _attn_kernel(qt_ref, k_ref, vt_ref, o_ref, acc_sc, s_even, s_odd):
    tq = ROW_TILE
    n_seq = k_ref.shape[1] - 1
    is_seq = pl.program_id(1) > 0
    i = pl.program_id(1) - 1
    key_idx = lax.broadcasted_iota(jnp.int32, (tq, tq), 0)
    qry_idx = lax.broadcasted_iota(jnp.int32, (tq, tq), 1)
    causal = key_idx <= qry_idx
    ones_rows = jnp.ones((SUM_ROWS, tq), BF16)

    def score_head(s_ref, j, hd):
        hsl = slice(hd * HEAD_PAD, (hd + 1) * HEAD_PAD)
        s_ref[hd] = _dot(k_ref[0, j, :, hsl], qt_ref[0, hsl, :])

    def scores_into(s_ref, j):
        for hd in range(MLA_HEADS):
            score_head(s_ref, j, hd)

    mkey = lax.broadcasted_iota(jnp.int32, (N_META, tq), 0)
    mqry = lax.broadcasted_iota(jnp.int32, (N_META, tq), 1)
    meta_ok = is_seq | (mkey <= mqry)
    meta_scores = []
    for hd in range(MLA_HEADS):
        hsl = slice(hd * HEAD_PAD, (hd + 1) * HEAD_PAD)
        meta_scores.append(_dot(k_ref[0, n_seq, 0:N_META, hsl], qt_ref[0, hsl, :]))
    scores_into(s_even, 0)
    zero_rows = jnp.zeros((META_KEYS - N_META, tq), BF16)
    maxes = []
    for hd in range(MLA_HEADS):
        vsl = slice(hd * V_DIM, (hd + 1) * V_DIM)
        s = jnp.where(meta_ok, meta_scores[hd], -jnp.inf)
        m0 = jnp.max(s, axis=0, keepdims=True)
        p = jnp.concatenate([jnp.exp2(s - m0).astype(BF16), zero_rows], axis=0)
        v_ext = jnp.concatenate([vt_ref[0, n_seq, vsl, 0:META_KEYS], ones_rows[:, 0:META_KEYS]], axis=0)
        acc_sc[hd] = _dot(v_ext, p)
        maxes.append(m0)
    maxes = tuple(maxes)

    def consume_head(s_ref, j, hd, m_old, masked):
        vsl = slice(hd * V_DIM, (hd + 1) * V_DIM)
        s = s_ref[hd]
        if masked:
            s = jnp.where(causal, s, -jnp.inf)
        m_new = jnp.maximum(m_old, jnp.max(s, axis=0, keepdims=True))
        alpha = jnp.exp2(m_old - m_new)
        p = jnp.exp2(s - m_new).astype(BF16)
        v_ext = jnp.concatenate([vt_ref[0, j, vsl, :], ones_rows], axis=0)
        acc_sc[hd] = alpha * acc_sc[hd] + _dot(v_ext, p)
        return m_new

    def consume(s_ref, j, maxes, masked, s_next=None, j_next=None):
        new_maxes = []
        if s_next is not None:
            for hd in range(QK_LEAD):
                score_head(s_next, j_next, hd)
        for hd in range(MLA_HEADS):
            new_maxes.append(consume_head(s_ref, j, hd, maxes[hd], masked))
            if s_next is not None and hd + QK_LEAD < MLA_HEADS:
                score_head(s_next, j_next, hd + QK_LEAD)
        return tuple(new_maxes)

    def finish():
        for pair in range(MLA_HEADS // 2):
            halves = []
            for hd in (2 * pair, 2 * pair + 1):
                acc = acc_sc[hd]
                halves.append(acc[0:V_DIM, :] / acc[V_DIM:V_DIM + 1, :])
            o_ref[0, :, pair * LANES:(pair + 1) * LANES] = jnp.concatenate(halves, axis=0).T.astype(BF16)

    def two_blocks(jj, maxes):
        j = 2 * jj
        maxes = consume(s_even, j, maxes, False, s_odd, j + 1)
        return consume(s_odd, j + 1, maxes, False, s_even, j + 2)

    def four_blocks(jj, maxes):
        return two_blocks(2 * jj + 1, two_blocks(2 * jj, maxes))

    n_full = jnp.where(is_seq, i, 0)
    maxes = lax.fori_loop(0, n_full // 4, four_blocks, maxes)
    maxes = lax.fori_loop(2 * (n_full // 4), n_full // 2, two_blocks, maxes)

    @pl.when(is_seq & (i % 2 == 0))
    def _():
        consume(s_even, i, maxes, True)
        finish()

    @pl.when(is_seq & (i % 2 == 1))
    def _():
        consume(s_odd, i, consume(s_even, i - 1, maxes, False, s_odd, i), True)
        finish()

    @pl.when(jnp.logical_not(is_seq))
    def _():
        finish()


def _attn_call(qt, k, vt):
    bsz, nt = k.shape[0], k.shape[1]
    lp = nt * ROW_TILE
    stored = lambda s: (s + nt - 1) % nt
    return pl.pallas_call(
        _attn_kernel,
        grid=(bsz, nt),
        in_specs=[pl.BlockSpec((1, QK_W, ROW_TILE), lambda b, s: (b, 0, stored(s))),
                  pl.BlockSpec((1, nt, ROW_TILE, QK_W), lambda b, s: (b, 0, 0, 0)),
                  pl.BlockSpec((1, nt, MLA_W, ROW_TILE), lambda b, s: (b, 0, 0, 0))],
        out_specs=pl.BlockSpec((1, ROW_TILE, MLA_W), lambda b, s: (b, stored(s), 0)),
        out_shape=jax.ShapeDtypeStruct((bsz, lp, MLA_W), BF16),
        scratch_shapes=[pltpu.VMEM((MLA_HEADS, V_DIM + SUM_ROWS, ROW_TILE), F32),
                        pltpu.VMEM((MLA_HEADS, ROW_TILE, ROW_TILE), F32),
                        pltpu.VMEM((MLA_HEADS, ROW_TILE, ROW_TILE), F32)],
        compiler_params=pltpu.CompilerParams(dimension_semantics=("arbitrary", "arbitrary"),
                                             vmem_limit_bytes=VMEM_LIMIT),
        name="mla_attention",
    )(qt, k, vt)


def _post_kernel(emit_h, x_ref, h_ref, u_ref, o_ref, smg_ref, wg_ref, gb_ref, wpool_ref, wmla_ref, wconf_ref,
                 wsc_ref, wo_ref, pog_ref, *rest):
    if emit_h:
        png_next_ref, out_ref, ho_ref = rest
    else:
        (out_ref,) = rest
    branch_w = (wpool_ref, wmla_ref, wconf_ref, wsc_ref)
    halves = [slice(r, r + POST_TILE // POST_SPLIT) for r in range(0, POST_TILE, POST_TILE // POST_SPLIT)]

    def merged(rows):
        h = h_ref[0, rows, :]
        u = u_ref[0, rows, :]
        ub = (o_ref[0, rows, :].astype(F32) * smg_ref[0, rows, :].astype(F32)).astype(BF16)
        branch_in = (u[:, 0:POOL_W], ub, u[:, POOL_W:POOL_W + CONF_W], u[:, POOL_W + CONF_W:])
        m = None
        for br in range(N_BRANCH):
            csl = slice(br * D_MODEL, (br + 1) * D_MODEL)
            gate = jax.nn.sigmoid(_dot(h, wg_ref[:, csl]) + gb_ref[:, csl])
            y = gate * _dot(branch_in[br], branch_w[br][...])
            m = y if m is None else m + y
        return m.astype(BF16)

    ms = [merged(rows) for rows in halves]
    mos = [_dot(m, wo_ref[...]) for m in ms]
    for rows, mo in zip(halves, mos):
        x_new = x_ref[0, rows, :] + _rms(mo, pog_ref[...])
        out_ref[0, rows, :] = x_new
        if emit_h:
            ho_ref[0, rows, :] = _rms(x_new, png_next_ref[...]).astype(BF16)


def _post_call(x, h, u, o, smg, layer, wts, rows_out, emit_h):
    groups = x.shape[0]
    row_spec = lambda w: pl.BlockSpec((1, POST_TILE, w), lambda g, r: (g, r, 0))
    consts = [wts[n] for n in ("wg", "gb", "wpool", "wmla", "wconf", "wsc", "wo", "pog")]
    const_specs = [_layer_spec(c, layer, single_buffer=True) for c in consts]
    out_specs = [row_spec(D_MODEL)]
    out_shape = [jax.ShapeDtypeStruct((groups, rows_out, D_MODEL), F32)]
    if emit_h:
        consts.append(wts["png"])
        const_specs.append(_layer_spec(wts["png"], layer + 1, single_buffer=True))
        out_specs.append(row_spec(D_MODEL))
        out_shape.append(jax.ShapeDtypeStruct((groups, rows_out, D_MODEL), BF16))
    return pl.pallas_call(
        functools.partial(_post_kernel, emit_h),
        grid=(groups, rows_out // POST_TILE),
        in_specs=[row_spec(D_MODEL), row_spec(D_MODEL), row_spec(U_W), row_spec(MLA_W), row_spec(MLA_W)]
        + const_specs,
        out_specs=out_specs,
        out_shape=out_shape,
        compiler_params=pltpu.CompilerParams(dimension_semantics=("arbitrary", "arbitrary"),
                                             vmem_limit_bytes=VMEM_LIMIT),
        name="mixer_post",
    )(x, h, u, o, smg, *consts)


def _rope_tables(seq):
    inv = 1.0 / (ROPE_THETA ** (jnp.arange(0, QK_ROPE, 2, dtype=F32) / QK_ROPE))
    pos = jnp.concatenate([N_META + jnp.arange(seq, dtype=F32), jnp.arange(ROW_TILE, dtype=F32)])
    ang = pos[:, None] * inv[None, :]
    cos, sin = jnp.cos(ang), jnp.sin(ang)
    n = pos.shape[0]
    zeros = lambda w: jnp.zeros((n, w), F32)
    tail = HEAD_PAD - QK_DIM
    ck = jnp.concatenate([zeros(QK_NOPE), cos, cos, zeros(tail)], axis=1)
    sk = jnp.concatenate([zeros(QK_NOPE), -sin, sin, zeros(tail)], axis=1)
    return (cos * Q_SCALE).T, (sin * Q_SCALE).T, ck, sk


def _swap_halves(w):
    half = QK_ROPE // 2
    return jnp.concatenate([w[..., half:], w[..., :half]], axis=-1)


def _prepare_weights(p):
    w_in = p["w_in"]
    cuts = [0] + [int(s) for s in np.cumsum(IN_SPLITS)]
    (w_pv, w_pg, w_cq, w_ckv, w_kr, w_mg, w_cu, w_cg, w_sb, w_sg, w_gl) = [
        w_in[:, :, a:b] for a, b in zip(cuts[:-1], cuts[1:])]
    kr_tile = jnp.concatenate([jnp.zeros((DEPTH, D_MODEL, QK_NOPE), F32), w_kr, _swap_halves(w_kr)], axis=-1)
    w1 = jnp.concatenate([w_pv, w_pg, w_cq, w_ckv, kr_tile, w_mg, w_cu, w_cg, w_sb, w_sg], axis=-1)

    pbd = jnp.zeros((DEPTH, POOL_W, POOL_W), F32)
    for g in range(len(POOL_WINDOWS)):
        sl = slice(g * POOL_GROUP, (g + 1) * POOL_GROUP)
        pbd = pbd.at[:, sl, sl].set(p["pool_w"][:, g])

    wq_t = jnp.swapaxes(p["w_uq"], 1, 2)
    wukv = p["w_ukv"].reshape(DEPTH, KV_RANK, MLA_HEADS, QK_NOPE + V_DIM)
    wk = jnp.concatenate([wukv[..., :QK_NOPE], jnp.zeros(wukv.shape[:3] + (HEAD_PAD - QK_NOPE,), F32)],
                         axis=-1).reshape(DEPTH, KV_RANK, QK_W)
    wv_t = jnp.swapaxes(wukv[..., QK_NOPE:].reshape(DEPTH, KV_RANK, MLA_W), 1, 2)
    row = lambda a: a[:, None, :]
    bf = lambda a: a.astype(BF16)
    return dict(
        png=row(p["pre_norm_g"]), w1=bf(w1), pbd=bf(pbd), psc=row(p["pool_scale"]), qng=row(p["q_norm_g"]),
        wq=bf(wq_t), kvg=row(p["kv_norm_g"]), wk=bf(wk), wv=bf(wv_t), cw=p["conf_dw_w"],
        cb=row(p["conf_dw_b"]), lg=row(p["conf_ln_g"]), lb=row(p["conf_ln_b"]), sw=p["sc_dw_w"],
        wg=bf(w_gl), gb=row(p["gate_bias"]), wpool=bf(p["w_out_pool"]), wmla=bf(p["w_out_mla"]),
        wconf=bf(p["w_out_conf"]), wsc=bf(p["w_out_sc"]), wo=bf(p["w_o"]), pog=row(p["post_norm_g"]))


def kernel(x, meta_tokens, pre_norm_g, w_in, gate_bias, pool_w, pool_scale, w_out_pool, q_norm_g, w_uq,
           kv_norm_g, w_ukv, w_out_mla, conf_dw_w, conf_dw_b, conf_ln_g, conf_ln_b, w_out_conf, sc_dw_w,
           w_out_sc, w_o, post_norm_g):
    bsz, seq, _ = x.shape
    assert seq % POST_TILE == 0 and meta_tokens.shape[0] == N_META
    n_seq_tiles = seq // ROW_TILE
    lp = seq + ROW_TILE
    assert (bsz * lp) % POST_TILE == 0
    wts = _prepare_weights(dict(
        w_in=w_in, pre_norm_g=pre_norm_g, gate_bias=gate_bias, pool_w=pool_w, pool_scale=pool_scale,
        w_out_pool=w_out_pool, q_norm_g=q_norm_g, w_uq=w_uq, kv_norm_g=kv_norm_g, w_ukv=w_ukv,
        w_out_mla=w_out_mla, conf_dw_w=conf_dw_w, conf_dw_b=conf_dw_b, conf_ln_g=conf_ln_g,
        conf_ln_b=conf_ln_b, w_out_conf=w_out_conf, sc_dw_w=sc_dw_w, w_out_sc=w_out_sc, w_o=w_o,
        post_norm_g=post_norm_g))
    tabs = _rope_tables(seq)
    meta_tile = jnp.concatenate([meta_tokens.astype(x.dtype),
                                 jnp.zeros((ROW_TILE - N_META, D_MODEL), x.dtype)])[None]

    x_res = h_norm = None
    for layer in range(DEPTH):
        if layer == 0:
            qt, k, vt, smg, u, x_res, h_norm = _pre_call(True, (x, meta_tile), layer, wts, tabs, bsz,
                                                         n_seq_tiles)
        else:
            qt, k, vt, smg, u = _pre_call(False, (h_norm,), layer, wts, tabs, bsz, n_seq_tiles)
        o = _attn_call(qt, k, vt)
        if layer < DEPTH - 1:
            flat = lambda a: a.reshape(1, bsz * lp, a.shape[-1])
            x_res, h_norm = _post_call(flat(x_res), flat(h_norm), flat(u), flat(o), flat(smg), layer, wts,
                                       bsz * lp, True)
            x_res, h_norm = x_res.reshape(bsz, lp, D_MODEL), h_norm.reshape(bsz, lp, D_MODEL)
        else:
            (x_res,) = _post_call(x_res, h_norm, u, o, smg, layer, wts, seq, False)
    return x_res
```

```python
import functools

import jax
import jax.numpy as jnp
import numpy as np
from jax import lax
from jax.experimental import pallas as pl
from jax.experimental.pallas import tpu as pltpu

F32 = jnp.float32
BF16 = jnp.bfloat16

D_MODEL = 1024
DEPTH = 4
N_META = 16
EPS = 1e-6
N_BRANCH = 4
POOL_WINDOWS = (2, 4, 8, 16)
POOL_GROUP = 64
POOL_W = 256
MLA_HEADS = 8
QK_NOPE = 64
QK_ROPE = 32
QK_DIM = QK_NOPE + QK_ROPE
V_DIM = 64
Q_RANK = 256
KV_RANK = 128
ROPE_THETA = 10000.0
MLA_W = MLA_HEADS * V_DIM
CONF_W = 256
CONF_K = 31
SC_W = 256
SC_K = 3
IN_SPLITS = (POOL_W, POOL_W, Q_RANK, KV_RANK, QK_ROPE, MLA_W, 2 * CONF_W, CONF_W, 3 * SC_W, SC_W,
             N_BRANCH * D_MODEL)
U_W = POOL_W + CONF_W + SC_W

LANES = 128
SUBLANES = 8
HEAD_PAD = LANES
QK_W = MLA_HEADS * HEAD_PAD
ROW_TILE = 256
POST_TILE = 2 * ROW_TILE
POST_SPLIT = 2
POOL_HALO = 16
CONF_HALO = 32
SC_HALO = 8
SUM_ROWS = 16
QK_LEAD = 3
META_KEYS = LANES
VMEM_LIMIT = 56 * 1024 * 1024
LOG2_E = 1.4426950408889634
Q_SCALE = QK_DIM ** -0.5 * LOG2_E

C_PV, C_PG, C_CQ, C_CKV, C_KR, C_MG, C_CU, C_CG, C_SB, C_SG, C_END = (
    0, 256, 512, 768, 896, 1024, 1536, 2048, 2304, 3072, 3328)


def _rms(x, g):
    return x * lax.rsqrt(jnp.mean(x * x, axis=-1, keepdims=True) + EPS) * g


def _dot(a, b):
    return jnp.dot(a, b, preferred_element_type=F32)


def _dot_nt(a, b):
    return lax.dot_general(a, b, (((1,), (1,)), ((), ())), preferred_element_type=F32)


def _shifted_rows(buf, shifted, off, rows, lanes):
    tile, sh = divmod(off, SUBLANES)
    if sh == 0:
        return buf[off:off + rows, lanes]
    return shifted[sh - 1, tile * SUBLANES:tile * SUBLANES + rows, lanes]


def _pre_kernel(first, *refs):
    if first:
        x_ref, xm_ref, png_ref = refs[:3]
        refs = refs[3:]
    else:
        h_ref = refs[0]
        refs = refs[1:]
    (w1_ref, pbd_ref, psc_ref, qng_ref, wq_ref, kvg_ref, wk_ref, wv_ref, qc_ref, qs_ref, ck_ref, sk_ref,
     cw_ref, cb_ref, lg_ref, lb_ref, sw_ref, q_ref, k_ref, v_ref, smg_ref, u_ref) = refs[:22]
    refs = refs[22:]
    if first:
        xo_ref, ho_ref = refs[:2]
        refs = refs[2:]
    pbuf, cbuf, sbuf, pshift, cshift = refs
    tm = ROW_TILE
    t = pl.program_id(1)
    is_meta = t == 0

    @pl.when(is_meta)
    def _():
        pbuf[0:POOL_HALO, :] = jnp.zeros((POOL_HALO, POOL_W), F32)
        cbuf[0:CONF_HALO, :] = jnp.zeros((CONF_HALO, CONF_W), F32)
        sbuf[0:SC_HALO, :] = jnp.zeros((SC_HALO, SC_W), F32)

    if first:
        x = jnp.where(is_meta, xm_ref[0], x_ref[0])
        xo_ref[0] = x
        ho_ref[0] = _rms(x, png_ref[...]).astype(BF16)
        h_ref = ho_ref
    valid = pl.multiple_of(jnp.where(is_meta, N_META, tm), SUBLANES)

    def proj(lo, hi):
        return _dot(h_ref[0], w1_ref[:, lo:hi])

    zu = proj(C_CU, C_SB)
    glu = zu[:, 0:CONF_W] * jax.nn.sigmoid(zu[:, CONF_W:2 * CONF_W])
    cbuf[CONF_HALO:CONF_HALO + tm, :] = glu
    for sh in range(1, SUBLANES):
        cshift[sh - 1] = cbuf[sh:sh + CONF_HALO + tm - SUBLANES, :]

    def conv_taps(acc, k_lo, k_hi):
        base = CONF_HALO - (CONF_K - 1)
        for kk in range(k_lo, k_hi):
            acc = acc + cw_ref[kk:kk + 1, :] * _shifted_rows(cbuf, cshift, base + kk, tm, slice(None))
        return acc

    tap_cuts = (0, 8, 16, 24, CONF_K)
    acc = conv_taps(jnp.zeros((tm, CONF_W), F32) + cb_ref[...], tap_cuts[0], tap_cuts[1])
    zp = proj(C_PV, C_CQ)
    acc = conv_taps(acc, tap_cuts[1], tap_cuts[2])
    zc = proj(C_CQ, C_MG)
    cqn = _rms(zc[:, 0:Q_RANK], qng_ref[...]).astype(BF16)
    ckvn = _rms(zc[:, Q_RANK:Q_RANK + KV_RANK], kvg_ref[...]).astype(BF16)
    zs = proj(C_SB, C_END)
    acc = conv_taps(acc, tap_cuts[2], tap_cuts[3])
    qa = _dot_nt(wq_ref[...], cqn)
    kn = _dot(ckvn, wk_ref[...])
    vt = _dot_nt(wv_ref[...], ckvn)
    zm = proj(C_MG, C_CU)
    acc = conv_taps(acc, tap_cuts[3], tap_cuts[4])
    mu = jnp.mean(acc, axis=-1, keepdims=True)
    cen = acc - mu
    var = jnp.mean(cen * cen, axis=-1, keepdims=True)
    yc = cen * lax.rsqrt(var + EPS) * lg_ref[...] + lb_ref[...]
    yc = jax.nn.silu(yc) * jax.nn.silu(zu[:, 2 * CONF_W:3 * CONF_W])
    u_ref[0, :, POOL_W:POOL_W + CONF_W] = yc.astype(BF16)
    cbuf[0:CONF_HALO, :] = cbuf[pl.ds(valid, CONF_HALO), :]

    pv = zp[:, :POOL_W]
    pbuf[POOL_HALO:POOL_HALO + tm, :] = pv
    first_pos = jnp.where(is_meta, 0, N_META + (t - 1) * tm)
    pos = lax.broadcasted_iota(jnp.int32, (tm, LANES), 0) + first_pos
    low_half = lax.broadcasted_iota(jnp.int32, (tm, LANES), 1) < POOL_GROUP

    def window_sum(col, acc, j_lo, j_hi):
        for j in range(j_lo, j_hi):
            acc = acc + _shifted_rows(pbuf, pshift, POOL_HALO - j, tm, slice(col, col + LANES))
        return acc

    pooled = []
    for tile, (w_lo, w_hi) in enumerate(((POOL_WINDOWS[0], POOL_WINDOWS[1]),
                                         (POOL_WINDOWS[2], POOL_WINDOWS[3]))):
        col = tile * LANES
        lanes = slice(col, col + LANES)
        for sh in sorted({(POOL_HALO - j) % SUBLANES for j in range(1, w_hi)} - {0}):
            pshift[sh - 1, :, lanes] = pbuf[sh:sh + POOL_HALO + tm - SUBLANES, lanes]
        self_v = pv[:, col:col + LANES]
        s_lo = window_sum(col, self_v, 1, w_lo)
        s_hi = window_sum(col, s_lo, w_lo, w_hi)
        width = jnp.where(low_half, w_lo, w_hi)
        cnt = jnp.minimum(pos + 1, width).astype(F32)
        pooled.append(jnp.where(low_half, s_lo, s_hi) / cnt - self_v)
    p = jnp.concatenate(pooled, axis=1).astype(BF16)
    ya = _dot(p, pbd_ref[...]) * psc_ref[...] * jax.nn.silu(zp[:, POOL_W:])
    u_ref[0, :, 0:POOL_W] = ya.astype(BF16)
    pbuf[0:POOL_HALO, :] = pbuf[pl.ds(valid, POOL_HALO), :]

    qc = qc_ref[...]
    qs = qs_ref[...]
    half = QK_ROPE // 2
    for hd in range(MLA_HEADS):
        src = hd * QK_DIM
        dst = hd * HEAD_PAD
        t1 = qa[src + QK_NOPE:src + QK_NOPE + half, :]
        t2 = qa[src + QK_NOPE + half:src + QK_DIM, :]
        q_ref[0, dst:dst + QK_NOPE, :] = (qa[src:src + QK_NOPE, :] * Q_SCALE).astype(BF16)
        q_ref[0, dst + QK_NOPE:dst + QK_NOPE + half, :] = (t1 * qc - t2 * qs).astype(BF16)
        q_ref[0, dst + QK_NOPE + half:dst + QK_DIM, :] = (t1 * qs + t2 * qc).astype(BF16)
        q_ref[0, dst + QK_DIM:dst + HEAD_PAD, :] = jnp.zeros((HEAD_PAD - QK_DIM, tm), BF16)
    zk = zc[:, C_KR - C_CQ:C_MG - C_CQ]
    k_rope = zk * ck_ref[...] + pltpu.roll(zk, LANES - QK_ROPE, 1) * sk_ref[...]
    for hd in range(MLA_HEADS):
        sl = slice(hd * HEAD_PAD, (hd + 1) * HEAD_PAD)
        k_ref[0, 0, :, sl] = (kn[:, sl] + k_rope).astype(BF16)
    v_ref[0, 0] = vt.astype(BF16)
    smg_ref[0] = jax.nn.silu(zm).astype(BF16)

    sbuf[SC_HALO:SC_HALO + tm, :] = zs[:, SC_W:2 * SC_W] * zs[:, 2 * SC_W:3 * SC_W]
    conv = jnp.zeros((tm, SC_W), F32)
    base = SC_HALO - (SC_K - 1)
    for kk in range(SC_K):
        conv = conv + sw_ref[kk:kk + 1, :] * sbuf[base + kk:base + kk + tm, :]
    yd = zs[:, 0:SC_W] * conv * jax.nn.silu(zs[:, 3 * SC_W:4 * SC_W])
    u_ref[0, :, POOL_W + CONF_W:U_W] = yd.astype(BF16)
    sbuf[0:SC_HALO, :] = sbuf[pl.ds(valid, SC_HALO), :]


def _layer_spec(arr, layer, single_buffer=False):
    mode = pl.Buffered(1) if single_buffer else None
    return pl.BlockSpec((None,) + arr.shape[1:], lambda *_: (layer, 0, 0), pipeline_mode=mode)


def _pre_call(first, acts, layer, wts, tabs, bsz, n_seq_tiles):
    nt = n_seq_tiles + 1
    lp = nt * ROW_TILE
    stored = lambda t: (t + n_seq_tiles) % nt
    row_spec = lambda w: pl.BlockSpec((1, ROW_TILE, w), lambda b, t: (b, stored(t), 0))
    consts = [wts[n] for n in ("w1", "pbd", "psc", "qng", "wq", "kvg", "wk", "wv")]
    consts2 = [wts[n] for n in ("cw", "cb", "lg", "lb", "sw")]
    if first:
        consts = [wts["png"]] + consts
        act_specs = [pl.BlockSpec((1, ROW_TILE, D_MODEL), lambda b, t: (b, jnp.maximum(t - 1, 0), 0)),
                     pl.BlockSpec((1, ROW_TILE, D_MODEL), lambda b, t: (0, 0, 0))]
    else:
        act_specs = [row_spec(D_MODEL)]
    in_specs = (act_specs + [_layer_spec(c, layer) for c in consts]
                + [pl.BlockSpec((QK_ROPE // 2, ROW_TILE), lambda b, t: (0, stored(t)))] * 2
                + [pl.BlockSpec((ROW_TILE, LANES), lambda b, t: (stored(t), 0))] * 2
                + [_layer_spec(c, layer) for c in consts2])
    out_specs = [pl.BlockSpec((1, QK_W, ROW_TILE), lambda b, t: (b, 0, stored(t))),
                 pl.BlockSpec((1, 1, ROW_TILE, QK_W), lambda b, t: (b, stored(t), 0, 0)),
                 pl.BlockSpec((1, 1, MLA_W, ROW_TILE), lambda b, t: (b, stored(t), 0, 0)),
                 row_spec(MLA_W), row_spec(U_W)]
    out_shape = [jax.ShapeDtypeStruct((bsz, QK_W, lp), BF16),
                 jax.ShapeDtypeStruct((bsz, nt, ROW_TILE, QK_W), BF16),
                 jax.ShapeDtypeStruct((bsz, nt, MLA_W, ROW_TILE), BF16),
                 jax.ShapeDtypeStruct((bsz, lp, MLA_W), BF16),
                 jax.ShapeDtypeStruct((bsz, lp, U_W), BF16)]
    if first:
        out_specs += [row_spec(D_MODEL), row_spec(D_MODEL)]
        out_shape += [jax.ShapeDtypeStruct((bsz, lp, D_MODEL), F32),
                      jax.ShapeDtypeStruct((bsz, lp, D_MODEL), BF16)]
    return pl.pallas_call(
        functools.partial(_pre_kernel, first),
        grid=(bsz, nt),
        in_specs=in_specs,
        out_specs=out_specs,
        out_shape=out_shape,
        scratch_shapes=[pltpu.VMEM((POOL_HALO + ROW_TILE, POOL_W), F32),
                        pltpu.VMEM((CONF_HALO + ROW_TILE, CONF_W), F32),
                        pltpu.VMEM((SC_HALO + ROW_TILE, SC_W), F32),
                        pltpu.VMEM((SUBLANES - 1, POOL_HALO + ROW_TILE - SUBLANES, POOL_W), F32),
                        pltpu.VMEM((SUBLANES - 1, CONF_HALO + ROW_TILE - SUBLANES, CONF_W), F32)],
        compiler_params=pltpu.CompilerParams(dimension_semantics=("arbitrary", "arbitrary"),
                                             vmem_limit_bytes=VMEM_LIMIT),
        name="mixer_pre",
    )(*acts, *consts, *tabs, *consts2)


def _attn_kernel(qt_ref, qn_ref, k_ref, vt_ref, o_ref, acc_sc, s_even, s_odd, acc_next, m_next):
    tq = ROW_TILE
    n_seq = k_ref.shape[1] - 1
    is_seq = pl.program_id(1) > 0
    i = pl.program_id(1) - 1
    key_idx = lax.broadcasted_iota(jnp.int32, (tq, tq), 0)
    qry_idx = lax.broadcasted_iota(jnp.int32, (tq, tq), 1)
    causal = key_idx <= qry_idx
    ones_rows = jnp.ones((SUM_ROWS, tq), BF16)

    def score_head(s_ref, j, hd):
        hsl = slice(hd * HEAD_PAD, (hd + 1) * HEAD_PAD)
        s_ref[hd] = _dot(k_ref[0, j, :, hsl], qt_ref[0, hsl, :])

    def next_tile_head(hd):
        hsl = slice(hd * HEAD_PAD, (hd + 1) * HEAD_PAD)
        s_even[hd] = _dot(k_ref[0, 0, :, hsl], qn_ref[0, hsl, :])

    zero_rows = jnp.zeros((META_KEYS - N_META, tq), BF16)

    def meta_block(q_ref, causal_meta, acc_dst):
        mkey = lax.broadcasted_iota(jnp.int32, (N_META, tq), 0)
        mqry = lax.broadcasted_iota(jnp.int32, (N_META, tq), 1)
        scores = []
        for hd in range(MLA_HEADS):
            hsl = slice(hd * HEAD_PAD, (hd + 1) * HEAD_PAD)
            scores.append(_dot(k_ref[0, n_seq, 0:N_META, hsl], q_ref[0, hsl, :]))
        for hd in range(MLA_HEADS):
            vsl = slice(hd * V_DIM, (hd + 1) * V_DIM)
            s = jnp.where(mkey <= mqry, scores[hd], -jnp.inf) if causal_meta else scores[hd]
            m0 = jnp.max(s, axis=0, keepdims=True)
            p = jnp.concatenate([jnp.exp2(s - m0).astype(BF16), zero_rows], axis=0)
            v_ext = jnp.concatenate([vt_ref[0, n_seq, vsl, 0:META_KEYS], ones_rows[:, 0:META_KEYS]], axis=0)
            acc_dst[hd] = _dot(v_ext, p)
            m_next[hd] = jnp.broadcast_to(m0, (SUBLANES, tq))

    def prepare_next_tile():
        meta_block(qn_ref, False, acc_next)

    @pl.when(is_seq)
    def _():
        acc_sc[...] = acc_next[...]

    maxes = tuple(m_next[hd, 0:1, :] for hd in range(MLA_HEADS))

    def consume_head(s_ref, j, hd, m_old, masked):
        vsl = slice(hd * V_DIM, (hd + 1) * V_DIM)
        s = s_ref[hd]
        if masked:
            s = jnp.where(causal, s, -jnp.inf)
        m_new = jnp.maximum(m_old, jnp.max(s, axis=0, keepdims=True))
        alpha = jnp.exp2(m_old - m_new)
        p = jnp.exp2(s - m_new).astype(BF16)
        v_ext = jnp.concatenate([vt_ref[0, j, vsl, :], ones_rows], axis=0)
        acc_sc[hd] = alpha * acc_sc[hd] + _dot(v_ext, p)
        return m_new

    def consume(s_ref, j, maxes, masked, produce=None, lead=QK_LEAD):
        new_maxes = []
        if produce is not None:
            for hd in range(lead):
                produce(hd)
        for hd in range(MLA_HEADS):
            new_maxes.append(consume_head(s_ref, j, hd, maxes[hd], masked))
            if produce is not None and hd + lead < MLA_HEADS:
                produce(hd + lead)
        return tuple(new_maxes)

    def into(s_ref, j):
        return lambda hd: score_head(s_ref, j, hd)

    def finish():
        for pair in range(MLA_HEADS // 2):
            halves = []
            for hd in (2 * pair, 2 * pair + 1):
                acc = acc_sc[hd]
                halves.append(acc[0:V_DIM, :] / acc[V_DIM:V_DIM + 1, :])
            o_ref[0, :, pair * LANES:(pair + 1) * LANES] = jnp.concatenate(halves, axis=0).T.astype(BF16)

    def two_blocks(jj, maxes):
        j = 2 * jj
        maxes = consume(s_even, j, maxes, False, into(s_odd, j + 1))
        return consume(s_odd, j + 1, maxes, False, into(s_even, j + 2))

    def four_blocks(jj, maxes):
        return two_blocks(2 * jj + 1, two_blocks(2 * jj, maxes))

    n_full = jnp.where(is_seq, i, 0)
    maxes = lax.fori_loop(0, n_full // 4, four_blocks, maxes)
    maxes = lax.fori_loop(2 * (n_full // 4), n_full // 2, two_blocks, maxes)

    @pl.when(is_seq & (i % 2 == 0))
    def _():
        prepare_next_tile()
        consume(s_even, i, maxes, True, next_tile_head, lead=0)
        finish()

    @pl.when(is_seq & (i % 2 == 1))
    def _():
        m_mid = consume(s_even, i - 1, maxes, False, into(s_odd, i))
        prepare_next_tile()
        consume(s_odd, i, m_mid, True, next_tile_head)
        finish()

    @pl.when(jnp.logical_not(is_seq))
    def _():
        meta_block(qt_ref, True, acc_sc)
        for hd in range(MLA_HEADS):
            next_tile_head(hd)
        prepare_next_tile()
        finish()


def _attn_call(qt, k, vt):
    bsz, nt = k.shape[0], k.shape[1]
    lp = nt * ROW_TILE
    stored = lambda s: (s + nt - 1) % nt
    return pl.pallas_call(
        _attn_kernel,
        grid=(bsz, nt),
        in_specs=[pl.BlockSpec((1, QK_W, ROW_TILE), lambda b, s: (b, 0, stored(s))),
                  pl.BlockSpec((1, QK_W, ROW_TILE), lambda b, s: (b, 0, stored(jnp.minimum(s + 1, nt - 1)))),
                  pl.BlockSpec((1, nt, ROW_TILE, QK_W), lambda b, s: (b, 0, 0, 0)),
                  pl.BlockSpec((1, nt, MLA_W, ROW_TILE), lambda b, s: (b, 0, 0, 0))],
        out_specs=pl.BlockSpec((1, ROW_TILE, MLA_W), lambda b, s: (b, stored(s), 0)),
        out_shape=jax.ShapeDtypeStruct((bsz, lp, MLA_W), BF16),
        scratch_shapes=[pltpu.VMEM((MLA_HEADS, V_DIM + SUM_ROWS, ROW_TILE), F32),
                        pltpu.VMEM((MLA_HEADS, ROW_TILE, ROW_TILE), F32),
                        pltpu.VMEM((MLA_HEADS, ROW_TILE, ROW_TILE), F32),
                        pltpu.VMEM((MLA_HEADS, V_DIM + SUM_ROWS, ROW_TILE), F32),
                        pltpu.VMEM((MLA_HEADS, SUBLANES, ROW_TILE), F32)],
        compiler_params=pltpu.CompilerParams(dimension_semantics=("arbitrary", "arbitrary"),
                                             vmem_limit_bytes=VMEM_LIMIT),
        name="mla_attention",
    )(qt, qt, k, vt)


def _post_kernel(emit_h, x_ref, h_ref, u_ref, o_ref, smg_ref, wg_ref, gb_ref, wpool_ref, wmla_ref, wconf_ref,
                 wsc_ref, wo_ref, pog_ref, *rest):
    if emit_h:
        png_next_ref, out_ref, ho_ref = rest
    else:
        (out_ref,) = rest
    branch_w = (wpool_ref, wmla_ref, wconf_ref, wsc_ref)
    halves = [slice(r, r + POST_TILE // POST_SPLIT) for r in range(0, POST_TILE, POST_TILE // POST_SPLIT)]

    def merged(rows):
        h = h_ref[0, rows, :]
        u = u_ref[0, rows, :]
        ub = (o_ref[0, rows, :].astype(F32) * smg_ref[0, rows, :].astype(F32)).astype(BF16)
        branch_in = (u[:, 0:POOL_W], ub, u[:, POOL_W:POOL_W + CONF_W], u[:, POOL_W + CONF_W:])
        m = None
        for br in range(N_BRANCH):
            csl = slice(br * D_MODEL, (br + 1) * D_MODEL)
            gate = jax.nn.sigmoid(_dot(h, wg_ref[:, csl]) + gb_ref[:, csl])
            y = gate * _dot(branch_in[br], branch_w[br][...])
            m = y if m is None else m + y
        return m.astype(BF16)

    ms = [merged(rows) for rows in halves]
    mos = [_dot(m, wo_ref[...]) for m in ms]
    for rows, mo in zip(halves, mos):
        x_new = x_ref[0, rows, :] + _rms(mo, pog_ref[...])
        out_ref[0, rows, :] = x_new
        if emit_h:
            ho_ref[0, rows, :] = _rms(x_new, png_next_ref[...]).astype(BF16)


def _post_call(x, h, u, o, smg, layer, wts, rows_out, emit_h):
    groups = x.shape[0]
    row_spec = lambda w: pl.BlockSpec((1, POST_TILE, w), lambda g, r: (g, r, 0))
    consts = [wts[n] for n in ("wg", "gb", "wpool", "wmla", "wconf", "wsc", "wo", "pog")]
    const_specs = [_layer_spec(c, layer, single_buffer=True) for c in consts]
    out_specs = [row_spec(D_MODEL)]
    out_shape = [jax.ShapeDtypeStruct((groups, rows_out, D_MODEL), F32)]
    if emit_h:
        consts.append(wts["png"])
        const_specs.append(_layer_spec(wts["png"], layer + 1, single_buffer=True))
        out_specs.append(row_spec(D_MODEL))
        out_shape.append(jax.ShapeDtypeStruct((groups, rows_out, D_MODEL), BF16))
    return pl.pallas_call(
        functools.partial(_post_kernel, emit_h),
        grid=(groups, rows_out // POST_TILE),
        in_specs=[row_spec(D_MODEL), row_spec(D_MODEL), row_spec(U_W), row_spec(MLA_W), row_spec(MLA_W)]
        + const_specs,
        out_specs=out_specs,
        out_shape=out_shape,
        compiler_params=pltpu.CompilerParams(dimension_semantics=("arbitrary", "arbitrary"),
                                             vmem_limit_bytes=VMEM_LIMIT),
        name="mixer_post",
    )(x, h, u, o, smg, *consts)


def _rope_tables(seq):
    inv = 1.0 / (ROPE_THETA ** (jnp.arange(0, QK_ROPE, 2, dtype=F32) / QK_ROPE))
    pos = jnp.concatenate([N_META + jnp.arange(seq, dtype=F32), jnp.arange(ROW_TILE, dtype=F32)])
    ang = pos[:, None] * inv[None, :]
    cos, sin = jnp.cos(ang), jnp.sin(ang)
    n = pos.shape[0]
    zeros = lambda w: jnp.zeros((n, w), F32)
    tail = HEAD_PAD - QK_DIM
    ck = jnp.concatenate([zeros(QK_NOPE), cos, cos, zeros(tail)], axis=1)
    sk = jnp.concatenate([zeros(QK_NOPE), -sin, sin, zeros(tail)], axis=1)
    return (cos * Q_SCALE).T, (sin * Q_SCALE).T, ck, sk


def _swap_halves(w):
    half = QK_ROPE // 2
    return jnp.concatenate([w[..., half:], w[..., :half]], axis=-1)


def _prepare_weights(p):
    w_in = p["w_in"]
    cuts = [0] + [int(s) for s in np.cumsum(IN_SPLITS)]
    (w_pv, w_pg, w_cq, w_ckv, w_kr, w_mg, w_cu, w_cg, w_sb, w_sg, w_gl) = [
        w_in[:, :, a:b] for a, b in zip(cuts[:-1], cuts[1:])]
    kr_tile = jnp.concatenate([jnp.zeros((DEPTH, D_MODEL, QK_NOPE), F32), w_kr, _swap_halves(w_kr)], axis=-1)
    w1 = jnp.concatenate([w_pv, w_pg, w_cq, w_ckv, kr_tile, w_mg, w_cu, w_cg, w_sb, w_sg], axis=-1)

    pbd = jnp.zeros((DEPTH, POOL_W, POOL_W), F32)
    for g in range(len(POOL_WINDOWS)):
        sl = slice(g * POOL_GROUP, (g + 1) * POOL_GROUP)
        pbd = pbd.at[:, sl, sl].set(p["pool_w"][:, g])

    wq_t = jnp.swapaxes(p["w_uq"], 1, 2)
    wukv = p["w_ukv"].reshape(DEPTH, KV_RANK, MLA_HEADS, QK_NOPE + V_DIM)
    wk = jnp.concatenate([wukv[..., :QK_NOPE], jnp.zeros(wukv.shape[:3] + (HEAD_PAD - QK_NOPE,), F32)],
                         axis=-1).reshape(DEPTH, KV_RANK, QK_W)
    wv_t = jnp.swapaxes(wukv[..., QK_NOPE:].reshape(DEPTH, KV_RANK, MLA_W), 1, 2)
    row = lambda a: a[:, None, :]
    bf = lambda a: a.astype(BF16)
    return dict(
        png=row(p["pre_norm_g"]), w1=bf(w1), pbd=bf(pbd), psc=row(p["pool_scale"]), qng=row(p["q_norm_g"]),
        wq=bf(wq_t), kvg=row(p["kv_norm_g"]), wk=bf(wk), wv=bf(wv_t), cw=p["conf_dw_w"],
        cb=row(p["conf_dw_b"]), lg=row(p["conf_ln_g"]), lb=row(p["conf_ln_b"]), sw=p["sc_dw_w"],
        wg=bf(w_gl), gb=row(p["gate_bias"]), wpool=bf(p["w_out_pool"]), wmla=bf(p["w_out_mla"]),
        wconf=bf(p["w_out_conf"]), wsc=bf(p["w_out_sc"]), wo=bf(p["w_o"]), pog=row(p["post_norm_g"]))


def kernel(x, meta_tokens, pre_norm_g, w_in, gate_bias, pool_w, pool_scale, w_out_pool, q_norm_g, w_uq,
           kv_norm_g, w_ukv, w_out_mla, conf_dw_w, conf_dw_b, conf_ln_g, conf_ln_b, w_out_conf, sc_dw_w,
           w_out_sc, w_o, post_norm_g):
    bsz, seq, _ = x.shape
    assert seq % POST_TILE == 0 and meta_tokens.shape[0] == N_META
    n_seq_tiles = seq // ROW_TILE
    lp = seq + ROW_TILE
    assert (bsz * lp) % POST_TILE == 0
    wts = _prepare_weights(dict(
        w_in=w_in, pre_norm_g=pre_norm_g, gate_bias=gate_bias, pool_w=pool_w, pool_scale=pool_scale,
        w_out_pool=w_out_pool, q_norm_g=q_norm_g, w_uq=w_uq, kv_norm_g=kv_norm_g, w_ukv=w_ukv,
        w_out_mla=w_out_mla, conf_dw_w=conf_dw_w, conf_dw_b=conf_dw_b, conf_ln_g=conf_ln_g,
        conf_ln_b=conf_ln_b, w_out_conf=w_out_conf, sc_dw_w=sc_dw_w, w_out_sc=w_out_sc, w_o=w_o,
        post_norm_g=post_norm_g))
    tabs = _rope_tables(seq)
    meta_tile = jnp.concatenate([meta_tokens.astype(x.dtype),
                                 jnp.zeros((ROW_TILE - N_META, D_MODEL), x.dtype)])[None]

    x_res = h_norm = None
    for layer in range(DEPTH):
        if layer == 0:
            qt, k, vt, smg, u, x_res, h_norm = _pre_call(True, (x, meta_tile), layer, wts, tabs, bsz,
                                                         n_seq_tiles)
        else:
            qt, k, vt, smg, u = _pre_call(False, (h_norm,), layer, wts, tabs, bsz, n_seq_tiles)
        o = _attn_call(qt, k, vt)
        if layer < DEPTH - 1:
            flat = lambda a: a.reshape(1, bsz * lp, a.shape[-1])
            x_res, h_norm = _post_call(flat(x_res), flat(h_norm), flat(u), flat(o), flat(smg), layer, wts,
                                       bsz * lp, True)
            x_res, h_norm = x_res.reshape(bsz, lp, D_MODEL), h_norm.reshape(bsz, lp, D_MODEL)
        else:
            (x_res,) = _post_call(x_res, h_norm, u, o, smg, layer, wts, seq, False)
    return x_res
```

```python
import functools

import jax
import jax.numpy as jnp
import numpy as np
from jax import lax
from jax.experimental import pallas as pl
from jax.experimental.pallas import tpu as pltpu

F32 = jnp.float32
BF16 = jnp.bfloat16

D_MODEL = 1024
DEPTH = 4
N_META = 16
EPS = 1e-6
N_BRANCH = 4
POOL_WINDOWS = (2, 4, 8, 16)
POOL_GROUP = 64
POOL_W = 256
MLA_HEADS = 8
QK_NOPE = 64
QK_ROPE = 32
QK_DIM = QK_NOPE + QK_ROPE
V_DIM = 64
Q_RANK = 256
KV_RANK = 128
ROPE_THETA = 10000.0
MLA_W = MLA_HEADS * V_DIM
CONF_W = 256
CONF_K = 31
SC_W = 256
SC_K = 3
IN_SPLITS = (POOL_W, POOL_W, Q_RANK, KV_RANK, QK_ROPE, MLA_W, 2 * CONF_W, CONF_W, 3 * SC_W, SC_W,
             N_BRANCH * D_MODEL)
U_W = POOL_W + CONF_W + SC_W

LANES = 128
SUBLANES = 8
HEAD_PAD = LANES
QK_W = MLA_HEADS * HEAD_PAD
ROW_TILE = 256
POST_TILE = 2 * ROW_TILE
POST_SPLIT = 2
POOL_HALO = 16
CONF_HALO = 32
SC_HALO = 8
SUM_ROWS = 16
QK_LEAD = 3
META_KEYS = LANES
VMEM_LIMIT = 56 * 1024 * 1024
LOG2_E = 1.4426950408889634
Q_SCALE = QK_DIM ** -0.5 * LOG2_E

C_PV, C_PG, C_CQ, C_CKV, C_KR, C_MG, C_CU, C_CG, C_SB, C_SG, C_END = (
    0, 256, 512, 768, 896, 1024, 1536, 2048, 2304, 3072, 3328)


def _rms(x, g):
    return x * lax.rsqrt(jnp.mean(x * x, axis=-1, keepdims=True) + EPS) * g


def _dot(a, b):
    return jnp.dot(a, b, preferred_element_type=F32)


def _dot_nt(a, b):
    return lax.dot_general(a, b, (((1,), (1,)), ((), ())), preferred_element_type=F32)


def _shifted_rows(buf, shifted, off, rows, lanes):
    tile, sh = divmod(off, SUBLANES)
    if sh == 0:
        return buf[off:off + rows, lanes]
    return shifted[sh - 1, tile * SUBLANES:tile * SUBLANES + rows, lanes]


def _pre_kernel(first, *refs):
    if first:
        x_ref, xm_ref, png_ref = refs[:3]
        refs = refs[3:]
    else:
        h_ref = refs[0]
        refs = refs[1:]
    (w1_ref, pbd_ref, psc_ref, qng_ref, wq_ref, kvg_ref, wk_ref, wv_ref, qc_ref, qs_ref, ck_ref, sk_ref,
     cw_ref, cb_ref, lg_ref, lb_ref, sw_ref, q_ref, k_ref, v_ref, smg_ref, u_ref) = refs[:22]
    refs = refs[22:]
    if first:
        xo_ref, ho_ref = refs[:2]
        refs = refs[2:]
    pbuf, cbuf, sbuf, pshift, cshift = refs
    tm = ROW_TILE
    t = pl.program_id(1)
    is_meta = t == 0

    @pl.when(is_meta)
    def _():
        pbuf[0:POOL_HALO, :] = jnp.zeros((POOL_HALO, POOL_W), F32)
        cbuf[0:CONF_HALO, :] = jnp.zeros((CONF_HALO, CONF_W), F32)
        sbuf[0:SC_HALO, :] = jnp.zeros((SC_HALO, SC_W), F32)

    if first:
        x = jnp.where(is_meta, xm_ref[0], x_ref[0])
        xo_ref[0] = x
        ho_ref[0] = _rms(x, png_ref[...]).astype(BF16)
        h_ref = ho_ref
    valid = pl.multiple_of(jnp.where(is_meta, N_META, tm), SUBLANES)

    def proj(lo, hi):
        return _dot(h_ref[0], w1_ref[:, lo:hi])

    zu = proj(C_CU, C_SB)
    glu = zu[:, 0:CONF_W] * jax.nn.sigmoid(zu[:, CONF_W:2 * CONF_W])
    cbuf[CONF_HALO:CONF_HALO + tm, :] = glu
    for sh in range(1, SUBLANES):
        cshift[sh - 1] = cbuf[sh:sh + CONF_HALO + tm - SUBLANES, :]

    def conv_taps(acc, k_lo, k_hi):
        base = CONF_HALO - (CONF_K - 1)
        for kk in range(k_lo, k_hi):
            acc = acc + cw_ref[kk:kk + 1, :] * _shifted_rows(cbuf, cshift, base + kk, tm, slice(None))
        return acc

    tap_cuts = (0, 8, 16, 24, CONF_K)
    acc = conv_taps(jnp.zeros((tm, CONF_W), F32) + cb_ref[...], tap_cuts[0], tap_cuts[1])
    zp = proj(C_PV, C_CQ)
    acc = conv_taps(acc, tap_cuts[1], tap_cuts[2])
    zc = proj(C_CQ, C_MG)
    cqn = _rms(zc[:, 0:Q_RANK], qng_ref[...]).astype(BF16)
    ckvn = _rms(zc[:, Q_RANK:Q_RANK + KV_RANK], kvg_ref[...]).astype(BF16)
    zs = proj(C_SB, C_END)
    acc = conv_taps(acc, tap_cuts[2], tap_cuts[3])
    qa = _dot_nt(wq_ref[...], cqn)
    kn = _dot(ckvn, wk_ref[...])
    vt = _dot_nt(wv_ref[...], ckvn)
    zm = proj(C_MG, C_CU)
    acc = conv_taps(acc, tap_cuts[3], tap_cuts[4])
    mu = jnp.mean(acc, axis=-1, keepdims=True)
    cen = acc - mu
    var = jnp.mean(cen * cen, axis=-1, keepdims=True)
    yc = cen * lax.rsqrt(var + EPS) * lg_ref[...] + lb_ref[...]
    yc = jax.nn.silu(yc) * jax.nn.silu(zu[:, 2 * CONF_W:3 * CONF_W])
    u_ref[0, :, POOL_W:POOL_W + CONF_W] = yc.astype(BF16)
    cbuf[0:CONF_HALO, :] = cbuf[pl.ds(valid, CONF_HALO), :]

    pv = zp[:, :POOL_W]
    pbuf[POOL_HALO:POOL_HALO + tm, :] = pv
    first_pos = jnp.where(is_meta, 0, N_META + (t - 1) * tm)
    pos = lax.broadcasted_iota(jnp.int32, (tm, LANES), 0) + first_pos
    low_half = lax.broadcasted_iota(jnp.int32, (tm, LANES), 1) < POOL_GROUP

    def window_sum(col, acc, j_lo, j_hi):
        for j in range(j_lo, j_hi):
            acc = acc + _shifted_rows(pbuf, pshift, POOL_HALO - j, tm, slice(col, col + LANES))
        return acc

    pooled = []
    for tile, (w_lo, w_hi) in enumerate(((POOL_WINDOWS[0], POOL_WINDOWS[1]),
                                         (POOL_WINDOWS[2], POOL_WINDOWS[3]))):
        col = tile * LANES
        lanes = slice(col, col + LANES)
        for sh in sorted({(POOL_HALO - j) % SUBLANES for j in range(1, w_hi)} - {0}):
            pshift[sh - 1, :, lanes] = pbuf[sh:sh + POOL_HALO + tm - SUBLANES, lanes]
        self_v = pv[:, col:col + LANES]
        s_lo = window_sum(col, self_v, 1, w_lo)
        s_hi = window_sum(col, s_lo, w_lo, w_hi)
        width = jnp.where(low_half, w_lo, w_hi)
        cnt = jnp.minimum(pos + 1, width).astype(F32)
        pooled.append(jnp.where(low_half, s_lo, s_hi) / cnt - self_v)
    p = jnp.concatenate(pooled, axis=1).astype(BF16)
    ya = _dot(p, pbd_ref[...]) * psc_ref[...] * jax.nn.silu(zp[:, POOL_W:])
    u_ref[0, :, 0:POOL_W] = ya.astype(BF16)
    pbuf[0:POOL_HALO, :] = pbuf[pl.ds(valid, POOL_HALO), :]

    qc = qc_ref[...]
    qs = qs_ref[...]
    half = QK_ROPE // 2
    for hd in range(MLA_HEADS):
        src = hd * QK_DIM
        dst = hd * HEAD_PAD
        t1 = qa[src + QK_NOPE:src + QK_NOPE + half, :]
        t2 = qa[src + QK_NOPE + half:src + QK_DIM, :]
        q_ref[0, dst:dst + QK_NOPE, :] = (qa[src:src + QK_NOPE, :] * Q_SCALE).astype(BF16)
        q_ref[0, dst + QK_NOPE:dst + QK_NOPE + half, :] = (t1 * qc - t2 * qs).astype(BF16)
        q_ref[0, dst + QK_NOPE + half:dst + QK_DIM, :] = (t1 * qs + t2 * qc).astype(BF16)
        q_ref[0, dst + QK_DIM:dst + HEAD_PAD, :] = jnp.zeros((HEAD_PAD - QK_DIM, tm), BF16)
    zk = zc[:, C_KR - C_CQ:C_MG - C_CQ]
    k_rope = zk * ck_ref[...] + pltpu.roll(zk, LANES - QK_ROPE, 1) * sk_ref[...]
    for hd in range(MLA_HEADS):
        sl = slice(hd * HEAD_PAD, (hd + 1) * HEAD_PAD)
        k_ref[0, 0, :, sl] = (kn[:, sl] + k_rope).astype(BF16)
    v_ref[0, 0] = vt.astype(BF16)
    smg_ref[0] = jax.nn.silu(zm).astype(BF16)

    sbuf[SC_HALO:SC_HALO + tm, :] = zs[:, SC_W:2 * SC_W] * zs[:, 2 * SC_W:3 * SC_W]
    conv = jnp.zeros((tm, SC_W), F32)
    base = SC_HALO - (SC_K - 1)
    for kk in range(SC_K):
        conv = conv + sw_ref[kk:kk + 1, :] * sbuf[base + kk:base + kk + tm, :]
    yd = zs[:, 0:SC_W] * conv * jax.nn.silu(zs[:, 3 * SC_W:4 * SC_W])
    u_ref[0, :, POOL_W + CONF_W:U_W] = yd.astype(BF16)
    sbuf[0:SC_HALO, :] = sbuf[pl.ds(valid, SC_HALO), :]


def _layer_spec(arr, layer, single_buffer=False):
    mode = pl.Buffered(1) if single_buffer else None
    return pl.BlockSpec((None,) + arr.shape[1:], lambda *_: (layer, 0, 0), pipeline_mode=mode)


def _pre_call(first, acts, layer, wts, tabs, bsz, n_seq_tiles):
    nt = n_seq_tiles + 1
    lp = nt * ROW_TILE
    stored = lambda t: (t + n_seq_tiles) % nt
    row_spec = lambda w: pl.BlockSpec((1, ROW_TILE, w), lambda b, t: (b, stored(t), 0))
    consts = [wts[n] for n in ("w1", "pbd", "psc", "qng", "wq", "kvg", "wk", "wv")]
    consts2 = [wts[n] for n in ("cw", "cb", "lg", "lb", "sw")]
    if first:
        consts = [wts["png"]] + consts
        act_specs = [pl.BlockSpec((1, ROW_TILE, D_MODEL), lambda b, t: (b, jnp.maximum(t - 1, 0), 0)),
                     pl.BlockSpec((1, ROW_TILE, D_MODEL), lambda b, t: (0, 0, 0))]
    else:
        act_specs = [row_spec(D_MODEL)]
    in_specs = (act_specs + [_layer_spec(c, layer) for c in consts]
                + [pl.BlockSpec((QK_ROPE // 2, ROW_TILE), lambda b, t: (0, stored(t)))] * 2
                + [pl.BlockSpec((ROW_TILE, LANES), lambda b, t: (stored(t), 0))] * 2
                + [_layer_spec(c, layer) for c in consts2])
    out_specs = [pl.BlockSpec((1, QK_W, ROW_TILE), lambda b, t: (b, 0, stored(t))),
                 pl.BlockSpec((1, 1, ROW_TILE, QK_W), lambda b, t: (b, stored(t), 0, 0)),
                 pl.BlockSpec((1, 1, MLA_W, ROW_TILE), lambda b, t: (b, stored(t), 0, 0)),
                 row_spec(MLA_W), row_spec(U_W)]
    out_shape = [jax.ShapeDtypeStruct((bsz, QK_W, lp), BF16),
                 jax.ShapeDtypeStruct((bsz, nt, ROW_TILE, QK_W), BF16),
                 jax.ShapeDtypeStruct((bsz, nt, MLA_W, ROW_TILE), BF16),
                 jax.ShapeDtypeStruct((bsz, lp, MLA_W), BF16),
                 jax.ShapeDtypeStruct((bsz, lp, U_W), BF16)]
    if first:
        out_specs += [row_spec(D_MODEL), row_spec(D_MODEL)]
        out_shape += [jax.ShapeDtypeStruct((bsz, lp, D_MODEL), F32),
                      jax.ShapeDtypeStruct((bsz, lp, D_MODEL), BF16)]
    return pl.pallas_call(
        functools.partial(_pre_kernel, first),
        grid=(bsz, nt),
        in_specs=in_specs,
        out_specs=out_specs,
        out_shape=out_shape,
        scratch_shapes=[pltpu.VMEM((POOL_HALO + ROW_TILE, POOL_W), F32),
                        pltpu.VMEM((CONF_HALO + ROW_TILE, CONF_W), F32),
                        pltpu.VMEM((SC_HALO + ROW_TILE, SC_W), F32),
                        pltpu.VMEM((SUBLANES - 1, POOL_HALO + ROW_TILE - SUBLANES, POOL_W), F32),
                        pltpu.VMEM((SUBLANES - 1, CONF_HALO + ROW_TILE - SUBLANES, CONF_W), F32)],
        compiler_params=pltpu.CompilerParams(dimension_semantics=("arbitrary", "arbitrary"),
                                             vmem_limit_bytes=VMEM_LIMIT),
        name="mixer_pre",
    )(*acts, *consts, *tabs, *consts2)


def _attn_kernel(qt_ref, qn_ref, k_ref, vt_ref, o_ref, acc_sc, s_even, s_odd, acc_next, m_next):
    tq = ROW_TILE
    n_seq = k_ref.shape[1] - 1
    is_seq = pl.program_id(1) > 0
    i = pl.program_id(1) - 1
    key_idx = lax.broadcasted_iota(jnp.int32, (tq, tq), 0)
    qry_idx = lax.broadcasted_iota(jnp.int32, (tq, tq), 1)
    causal = key_idx <= qry_idx
    ones_rows = jnp.ones((SUM_ROWS, tq), BF16)

    def score_head(s_ref, j, hd):
        hsl = slice(hd * HEAD_PAD, (hd + 1) * HEAD_PAD)
        s_ref[hd] = _dot(k_ref[0, j, :, hsl], qt_ref[0, hsl, :])

    def next_tile_head(hd):
        hsl = slice(hd * HEAD_PAD, (hd + 1) * HEAD_PAD)
        s_even[hd] = _dot(k_ref[0, 0, :, hsl], qn_ref[0, hsl, :])

    zero_rows = jnp.zeros((META_KEYS - N_META, tq), BF16)

    def meta_block(q_ref, causal_meta, acc_dst):
        mkey = lax.broadcasted_iota(jnp.int32, (N_META, tq), 0)
        mqry = lax.broadcasted_iota(jnp.int32, (N_META, tq), 1)
        scores = []
        for hd in range(MLA_HEADS):
            hsl = slice(hd * HEAD_PAD, (hd + 1) * HEAD_PAD)
            scores.append(_dot(k_ref[0, n_seq, 0:N_META, hsl], q_ref[0, hsl, :]))
        for hd in range(MLA_HEADS):
            vsl = slice(hd * V_DIM, (hd + 1) * V_DIM)
            s = jnp.where(mkey <= mqry, scores[hd], -jnp.inf) if causal_meta else scores[hd]
            m0 = jnp.max(s, axis=0, keepdims=True)
            p = jnp.concatenate([jnp.exp2(s - m0).astype(BF16), zero_rows], axis=0)
            v_ext = jnp.concatenate([vt_ref[0, n_seq, vsl, 0:META_KEYS], ones_rows[:, 0:META_KEYS]], axis=0)
            acc_dst[hd] = _dot(v_ext, p)
            m_next[hd] = jnp.broadcast_to(m0, (SUBLANES, tq))

    def prepare_next_tile():
        meta_block(qn_ref, False, acc_next)

    @pl.when(is_seq)
    def _():
        acc_sc[...] = acc_next[...]

    @pl.when((pl.program_id(0) == 0) & (pl.program_id(1) == 0))
    def _():
        m_next[...] = jnp.zeros(m_next.shape, F32)

    maxes = tuple(m_next[hd, 0:1, :] for hd in range(MLA_HEADS))

    def consume_head(s_ref, j, hd, m_old, masked):
        vsl = slice(hd * V_DIM, (hd + 1) * V_DIM)
        s = s_ref[hd]
        if masked:
            s = jnp.where(causal, s, -jnp.inf)
        m_new = jnp.maximum(m_old, jnp.max(s, axis=0, keepdims=True))
        alpha = jnp.exp2(m_old - m_new)
        p = jnp.exp2(s - m_new).astype(BF16)
        v_ext = jnp.concatenate([vt_ref[0, j, vsl, :], ones_rows], axis=0)
        acc_sc[hd] = alpha * acc_sc[hd] + _dot(v_ext, p)
        return m_new

    def consume(s_ref, j, maxes, masked, produce=None, lead=QK_LEAD):
        new_maxes = []
        if produce is not None:
            for hd in range(lead):
                produce(hd)
        for hd in range(MLA_HEADS):
            new_maxes.append(consume_head(s_ref, j, hd, maxes[hd], masked))
            if produce is not None and hd + lead < MLA_HEADS:
                produce(hd + lead)
        return tuple(new_maxes)

    def into(s_ref, j):
        return lambda hd: score_head(s_ref, j, hd)

    def finish():
        for pair in range(MLA_HEADS // 2):
            halves = []
            for hd in (2 * pair, 2 * pair + 1):
                acc = acc_sc[hd]
                halves.append(acc[0:V_DIM, :] / acc[V_DIM:V_DIM + 1, :])
            o_ref[0, :, pair * LANES:(pair + 1) * LANES] = jnp.concatenate(halves, axis=0).T.astype(BF16)

    def two_blocks(jj, maxes):
        j = 2 * jj
        maxes = consume(s_even, j, maxes, False, into(s_odd, j + 1))
        return consume(s_odd, j + 1, maxes, False, into(s_even, j + 2))

    def four_blocks(jj, maxes):
        return two_blocks(2 * jj + 1, two_blocks(2 * jj, maxes))

    n_full = jnp.where(is_seq, i, 0)
    maxes = lax.fori_loop(0, n_full // 4, four_blocks, maxes)
    maxes = lax.fori_loop(2 * (n_full // 4), n_full // 2, two_blocks, maxes)

    @pl.when(is_seq & (i % 2 == 0))
    def _():
        prepare_next_tile()
        consume(s_even, i, maxes, True, next_tile_head, lead=0)
        finish()

    @pl.when(is_seq & (i % 2 == 1))
    def _():
        m_mid = consume(s_even, i - 1, maxes, False, into(s_odd, i))
        prepare_next_tile()
        consume(s_odd, i, m_mid, True, next_tile_head)
        finish()

    @pl.when(jnp.logical_not(is_seq))
    def _():
        meta_block(qt_ref, True, acc_sc)
        for hd in range(MLA_HEADS):
            next_tile_head(hd)
        prepare_next_tile()
        finish()


def _attn_call(qt, k, vt):
    bsz, nt = k.shape[0], k.shape[1]
    lp = nt * ROW_TILE
    stored = lambda s: (s + nt - 1) % nt
    return pl.pallas_call(
        _attn_kernel,
        grid=(bsz, nt),
        in_specs=[pl.BlockSpec((1, QK_W, ROW_TILE), lambda b, s: (b, 0, stored(s))),
                  pl.BlockSpec((1, QK_W, ROW_TILE), lambda b, s: (b, 0, stored(jnp.minimum(s + 1, nt - 1)))),
                  pl.BlockSpec((1, nt, ROW_TILE, QK_W), lambda b, s: (b, 0, 0, 0)),
                  pl.BlockSpec((1, nt, MLA_W, ROW_TILE), lambda b, s: (b, 0, 0, 0))],
        out_specs=pl.BlockSpec((1, ROW_TILE, MLA_W), lambda b, s: (b, stored(s), 0)),
        out_shape=jax.ShapeDtypeStruct((bsz, lp, MLA_W), BF16),
        scratch_shapes=[pltpu.VMEM((MLA_HEADS, V_DIM + SUM_ROWS, ROW_TILE), F32),
                        pltpu.VMEM((MLA_HEADS, ROW_TILE, ROW_TILE), F32),
                        pltpu.VMEM((MLA_HEADS, ROW_TILE, ROW_TILE), F32),
                        pltpu.VMEM((MLA_HEADS, V_DIM + SUM_ROWS, ROW_TILE), F32),
                        pltpu.VMEM((MLA_HEADS, SUBLANES, ROW_TILE), F32)],
        compiler_params=pltpu.CompilerParams(dimension_semantics=("arbitrary", "arbitrary"),
                                             vmem_limit_bytes=VMEM_LIMIT),
        name="mla_attention",
    )(qt, qt, k, vt)


def _post_kernel(emit_h, x_ref, h_ref, u_ref, o_ref, smg_ref, wg_ref, gb_ref, wpool_ref, wmla_ref, wconf_ref,
                 wsc_ref, wo_ref, pog_ref, *rest):
    if emit_h:
        png_next_ref, out_ref, ho_ref = rest
    else:
        (out_ref,) = rest
    branch_w = (wpool_ref, wmla_ref, wconf_ref, wsc_ref)
    halves = [slice(r, r + POST_TILE // POST_SPLIT) for r in range(0, POST_TILE, POST_TILE // POST_SPLIT)]

    def merged(rows):
        h = h_ref[0, rows, :]
        u = u_ref[0, rows, :]
        ub = (o_ref[0, rows, :].astype(F32) * smg_ref[0, rows, :].astype(F32)).astype(BF16)
        branch_in = (u[:, 0:POOL_W], ub, u[:, POOL_W:POOL_W + CONF_W], u[:, POOL_W + CONF_W:])
        m = None
        for br in range(N_BRANCH):
            csl = slice(br * D_MODEL, (br + 1) * D_MODEL)
            gate = jax.nn.sigmoid(_dot(h, wg_ref[:, csl]) + gb_ref[:, csl])
            y = gate * _dot(branch_in[br], branch_w[br][...])
            m = y if m is None else m + y
        return m.astype(BF16)

    ms = [merged(rows) for rows in halves]
    mos = [_dot(m, wo_ref[...]) for m in ms]
    for rows, mo in zip(halves, mos):
        x_new = x_ref[0, rows, :] + _rms(mo, pog_ref[...])
        out_ref[0, rows, :] = x_new
        if emit_h:
            ho_ref[0, rows, :] = _rms(x_new, png_next_ref[...]).astype(BF16)


def _post_call(x, h, u, o, smg, layer, wts, rows_out, emit_h):
    groups = x.shape[0]
    row_spec = lambda w: pl.BlockSpec((1, POST_TILE, w), lambda g, r: (g, r, 0))
    consts = [wts[n] for n in ("wg", "gb", "wpool", "wmla", "wconf", "wsc", "wo", "pog")]
    const_specs = [_layer_spec(c, layer, single_buffer=True) for c in consts]
    out_specs = [row_spec(D_MODEL)]
    out_shape = [jax.ShapeDtypeStruct((groups, rows_out, D_MODEL), F32)]
    if emit_h:
        consts.append(wts["png"])
        const_specs.append(_layer_spec(wts["png"], layer + 1, single_buffer=True))
        out_specs.append(row_spec(D_MODEL))
        out_shape.append(jax.ShapeDtypeStruct((groups, rows_out, D_MODEL), BF16))
    return pl.pallas_call(
        functools.partial(_post_kernel, emit_h),
        grid=(groups, rows_out // POST_TILE),
        in_specs=[row_spec(D_MODEL), row_spec(D_MODEL), row_spec(U_W), row_spec(MLA_W), row_spec(MLA_W)]
        + const_specs,
        out_specs=out_specs,
        out_shape=out_shape,
        compiler_params=pltpu.CompilerParams(dimension_semantics=("arbitrary", "arbitrary"),
                                             vmem_limit_bytes=VMEM_LIMIT),
        name="mixer_post",
    )(x, h, u, o, smg, *consts)


def _rope_tables(seq):
    inv = 1.0 / (ROPE_THETA ** (jnp.arange(0, QK_ROPE, 2, dtype=F32) / QK_ROPE))
    pos = jnp.concatenate([N_META + jnp.arange(seq, dtype=F32), jnp.arange(ROW_TILE, dtype=F32)])
    ang = pos[:, None] * inv[None, :]
    cos, sin = jnp.cos(ang), jnp.sin(ang)
    n = pos.shape[0]
    zeros = lambda w: jnp.zeros((n, w), F32)
    tail = HEAD_PAD - QK_DIM
    ck = jnp.concatenate([zeros(QK_NOPE), cos, cos, zeros(tail)], axis=1)
    sk = jnp.concatenate([zeros(QK_NOPE), -sin, sin, zeros(tail)], axis=1)
    return (cos * Q_SCALE).T, (sin * Q_SCALE).T, ck, sk


def _swap_halves(w):
    half = QK_ROPE // 2
    return jnp.concatenate([w[..., half:], w[..., :half]], axis=-1)


def _prepare_weights(p):
    w_in = p["w_in"]
    cuts = [0] + [int(s) for s in np.cumsum(IN_SPLITS)]
    (w_pv, w_pg, w_cq, w_ckv, w_kr, w_mg, w_cu, w_cg, w_sb, w_sg, w_gl) = [
        w_in[:, :, a:b] for a, b in zip(cuts[:-1], cuts[1:])]
    kr_tile = jnp.concatenate([jnp.zeros((DEPTH, D_MODEL, QK_NOPE), F32), w_kr, _swap_halves(w_kr)], axis=-1)
    w1 = jnp.concatenate([w_pv, w_pg, w_cq, w_ckv, kr_tile, w_mg, w_cu, w_cg, w_sb, w_sg], axis=-1)

    pbd = jnp.zeros((DEPTH, POOL_W, POOL_W), F32)
    for g in range(len(POOL_WINDOWS)):
        sl = slice(g * POOL_GROUP, (g + 1) * POOL_GROUP)
        pbd = pbd.at[:, sl, sl].set(p["pool_w"][:, g])

    wq_t = jnp.swapaxes(p["w_uq"], 1, 2)
    wukv = p["w_ukv"].reshape(DEPTH, KV_RANK, MLA_HEADS, QK_NOPE + V_DIM)
    wk = jnp.concatenate([wukv[..., :QK_NOPE], jnp.zeros(wukv.shape[:3] + (HEAD_PAD - QK_NOPE,), F32)],
                         axis=-1).reshape(DEPTH, KV_RANK, QK_W)
    wv_t = jnp.swapaxes(wukv[..., QK_NOPE:].reshape(DEPTH, KV_RANK, MLA_W), 1, 2)
    row = lambda a: a[:, None, :]
    bf = lambda a: a.astype(BF16)
    return dict(
        png=row(p["pre_norm_g"]), w1=bf(w1), pbd=bf(pbd), psc=row(p["pool_scale"]), qng=row(p["q_norm_g"]),
        wq=bf(wq_t), kvg=row(p["kv_norm_g"]), wk=bf(wk), wv=bf(wv_t), cw=p["conf_dw_w"],
        cb=row(p["conf_dw_b"]), lg=row(p["conf_ln_g"]), lb=row(p["conf_ln_b"]), sw=p["sc_dw_w"],
        wg=bf(w_gl), gb=row(p["gate_bias"]), wpool=bf(p["w_out_pool"]), wmla=bf(p["w_out_mla"]),
        wconf=bf(p["w_out_conf"]), wsc=bf(p["w_out_sc"]), wo=bf(p["w_o"]), pog=row(p["post_norm_g"]))


def kernel(x, meta_tokens, pre_norm_g, w_in, gate_bias, pool_w, pool_scale, w_out_pool, q_norm_g, w_uq,
           kv_norm_g, w_ukv, w_out_mla, conf_dw_w, conf_dw_b, conf_ln_g, conf_ln_b, w_out_conf, sc_dw_w,
           w_out_sc, w_o, post_norm_g):
    bsz, seq, _ = x.shape
    assert seq % POST_TILE == 0 and meta_tokens.shape[0] == N_META
    n_seq_tiles = seq // ROW_TILE
    lp = seq + ROW_TILE
    assert (bsz * lp) % POST_TILE == 0
    wts = _prepare_weights(dict(
        w_in=w_in, pre_norm_g=pre_norm_g, gate_bias=gate_bias, pool_w=pool_w, pool_scale=pool_scale,
        w_out_pool=w_out_pool, q_norm_g=q_norm_g, w_uq=w_uq, kv_norm_g=kv_norm_g, w_ukv=w_ukv,
        w_out_mla=w_out_mla, conf_dw_w=conf_dw_w, conf_dw_b=conf_dw_b, conf_ln_g=conf_ln_g,
        conf_ln_b=conf_ln_b, w_out_conf=w_out_conf, sc_dw_w=sc_dw_w, w_out_sc=w_out_sc, w_o=w_o,
        post_norm_g=post_norm_g))
    tabs = _rope_tables(seq)
    meta_tile = jnp.concatenate([meta_tokens.astype(x.dtype),
                                 jnp.zeros((ROW_TILE - N_META, D_MODEL), x.dtype)])[None]

    x_res = h_norm = None
    for layer in range(DEPTH):
        if layer == 0:
            qt, k, vt, smg, u, x_res, h_norm = _pre_call(True, (x, meta_tile), layer, wts, tabs, bsz,
                                                         n_seq_tiles)
        else:
            qt, k, vt, smg, u = _pre_call(False, (h_norm,), layer, wts, tabs, bsz, n_seq_tiles)
        o = _attn_call(qt, k, vt)
        if layer < DEPTH - 1:
            flat = lambda a: a.reshape(1, bsz * lp, a.shape[-1])
            x_res, h_norm = _post_call(flat(x_res), flat(h_norm), flat(u), flat(o), flat(smg), layer, wts,
                                       bsz * lp, True)
            x_res, h_norm = x_res.reshape(bsz, lp, D_MODEL), h_norm.reshape(bsz, lp, D_MODEL)
        else:
            (x_res,) = _post_call(x_res, h_norm, u, o, smg, layer, wts, seq, False)
    return x_res
```
